```python
import math
import jax, jax.numpy as jnp
from jax import lax
import numpy as np

D_MODEL = 2048
BATCH = 2
SEQ = 8192
DEPTH = 1

QK_NOPE_DIM = 128
QK_ROPE_DIM = 64
V_HEAD_DIM = 128
N_HEADS = D_MODEL // V_HEAD_DIM
Q_LORA_RANK = 512
KV_LORA_RANK = 512
QK_HEAD_DIM = QK_NOPE_DIM + QK_ROPE_DIM
ROPE_THETA = 10000.0
Q_BLOCK = 128

POOL_WINDOWS = (2, 4, 8, 16)
POOL_GROUPS = len(POOL_WINDOWS)
POOL_WIDTH = D_MODEL // 2
POOL_GROUP_DIM = POOL_WIDTH // POOL_GROUPS
POOL_OUT_GROUP_DIM = D_MODEL // POOL_GROUPS

OFF_Q = 0
OFF_KV = OFF_Q + Q_LORA_RANK
OFF_KR = OFF_KV + KV_LORA_RANK
OFF_POOL = OFF_KR + QK_ROPE_DIM
OFF_GA = OFF_POOL + POOL_WIDTH
OFF_GP = OFF_GA + D_MODEL
IN_PROJ_WIDTH = OFF_GP + D_MODEL

N_EXPERTS = 32
TOP_K = 4
D_FF = D_MODEL
SWIGLU_LIMIT = 7.0
SWIGLU_ALPHA = 1.702
EXPERT_BLOCK = 128

RMS_EPS = 1e-6
NEG_INF = -1e30

kernel_name = "hybrid_mla_pool_moe_layer"


def rms_norm(t, gain):
    t32 = t.astype(jnp.float32)
    t32 = t32 * lax.rsqrt(jnp.mean(t32 * t32, axis=-1, keepdims=True) + RMS_EPS)
    return (t32 * gain.astype(jnp.float32)).astype(t.dtype)


def rope(t, cos, sin):
    half = t.shape[-1] // 2
    t32 = t.astype(jnp.float32)
    t1, t2 = t32[..., :half], t32[..., half:]
    out = jnp.concatenate([t1 * cos - t2 * sin, t2 * cos + t1 * sin], axis=-1)
    return out.astype(t.dtype)


def mla_branch(zq, zkv, zkr, positions, g_q_latent, w_uq, g_kv_latent, w_ukv,
               g_q_nope, g_q_rope, g_k_nope, g_k_rope):
    B, S, _ = zq.shape
    q = rms_norm(zq, g_q_latent) @ w_uq
    q = q.reshape(B, S, N_HEADS, QK_HEAD_DIM)
    q_nope, q_pe = q[..., :QK_NOPE_DIM], q[..., QK_NOPE_DIM:]
    kv = rms_norm(zkv, g_kv_latent) @ w_ukv
    kv = kv.reshape(B, S, N_HEADS, QK_NOPE_DIM + V_HEAD_DIM)
    k_nope, v = kv[..., :QK_NOPE_DIM], kv[..., QK_NOPE_DIM:]
    q_nope = rms_norm(q_nope, g_q_nope)
    q_pe = rms_norm(q_pe, g_q_rope)
    k_nope = rms_norm(k_nope, g_k_nope)
    k_pe = rms_norm(zkr, g_k_rope)
    inv_freq = ROPE_THETA ** (-jnp.arange(0, QK_ROPE_DIM, 2, dtype=jnp.float32) / QK_ROPE_DIM)
    ang = positions.astype(jnp.float32)[..., None] * inv_freq
    cos, sin = jnp.cos(ang), jnp.sin(ang)
    q_pe = rope(q_pe, cos[:, :, None, :], sin[:, :, None, :])
    k_pe = rope(k_pe, cos, sin)

    scale = 1.0 / math.sqrt(QK_HEAD_DIM)
    n_qb = S // Q_BLOCK
    qn_blocks = q_nope.reshape(B, n_qb, Q_BLOCK, N_HEADS, QK_NOPE_DIM).transpose(1, 0, 2, 3, 4)
    qp_blocks = q_pe.reshape(B, n_qb, Q_BLOCK, N_HEADS, QK_ROPE_DIM).transpose(1, 0, 2, 3, 4)
    k_pos = jnp.arange(S)

    def attn_block(args):
        qn_b, qp_b, i = args
        s = jnp.einsum('bqhd,bkhd->bhqk', qn_b, k_nope, preferred_element_type=jnp.float32)
        s = s + jnp.einsum('bqhr,bkr->bhqk', qp_b, k_pe, preferred_element_type=jnp.float32)
        q_pos = i * Q_BLOCK + jnp.arange(Q_BLOCK)
        causal = k_pos[None, :] <= q_pos[:, None]
        s = jnp.where(causal, s * scale, NEG_INF)
        p = jax.nn.softmax(s, axis=-1).astype(v.dtype)
        return jnp.einsum('bhqk,bkhd->bqhd', p, v)

    o = lax.map(attn_block, (qn_blocks, qp_blocks, jnp.arange(n_qb)))
    return o.transpose(1, 0, 2, 3, 4).reshape(B, S, N_HEADS * V_HEAD_DIM)


def pool_branch(u, w_pool, s_pool):
    B, S, _ = u.shape
    ug = u.reshape(B, S, POOL_GROUPS, POOL_GROUP_DIM)
    cs = jnp.cumsum(ug.astype(jnp.float32), axis=1)
    cs0 = jnp.concatenate([jnp.zeros((B, 1, POOL_GROUPS, POOL_GROUP_DIM), jnp.float32), cs], axis=1)
    windows = jnp.array(POOL_WINDOWS, dtype=jnp.int32)
    hi = jnp.arange(1, S + 1, dtype=jnp.int32)[:, None]
    lo = jnp.maximum(hi - windows[None, :], 0)
    count = (hi - lo).astype(jnp.float32)
    g_idx = jnp.arange(POOL_GROUPS)[None, :]
    window_sum = cs - cs0[:, lo, g_idx]
    pooled = (window_sum / count[None, :, :, None]).astype(u.dtype) - ug
    y = jnp.einsum('bsgc,gcd->bsgd', pooled, w_pool).reshape(B, S, D_MODEL)
    return y * s_pool


def moe_ffn(h, w_router, b_router, w_gate_up, b_gate_up, w_down, b_down):
    T, D = h.shape
    logits = jnp.dot(h, w_router, preferred_element_type=jnp.float32) + b_router.astype(jnp.float32)
    top_vals, top_idx = lax.top_k(logits, TOP_K)
    gates = jax.nn.softmax(top_vals, axis=-1)
    A = T * TOP_K
    e_flat = top_idx.reshape(A)
    order = jnp.argsort(e_flat, stable=True)
    e_sorted = e_flat[order]
    tok_sorted = (order // TOP_K).astype(jnp.int32)
    w_sorted = gates.reshape(A)[order]
    counts = jnp.bincount(e_flat, length=N_EXPERTS)
    starts = jnp.cumsum(counts) - counts
    padded = (counts + EXPERT_BLOCK - 1) // EXPERT_BLOCK * EXPERT_BLOCK
    pends = jnp.cumsum(padded)
    pstarts = pends - padded
    dest = pstarts[e_sorted] + (jnp.arange(A) - starts[e_sorted])
    n_rows = A + N_EXPERTS * EXPERT_BLOCK
    n_blocks = (n_rows + EXPERT_BLOCK - 1) // EXPERT_BLOCK
    n_rows = n_blocks * EXPERT_BLOCK
    row_tok = jnp.full((n_rows,), T, jnp.int32).at[dest].set(tok_sorted)
    row_w = jnp.zeros((n_rows,), jnp.float32).at[dest].set(w_sorted)
    block_expert = jnp.minimum(
        jnp.searchsorted(pends, jnp.arange(n_blocks) * EXPERT_BLOCK, side='right'), N_EXPERTS - 1)
    h_pad = jnp.concatenate([h, jnp.zeros((1, D), h.dtype)], axis=0)

    def expert_block(args):
        tok_b, e = args
        xb = h_pad[tok_b]
        gu = xb @ w_gate_up[e] + b_gate_up[e]
        gate = jnp.minimum(gu[:, :D_FF], SWIGLU_LIMIT)
        up = jnp.clip(gu[:, D_FF:], -SWIGLU_LIMIT, SWIGLU_LIMIT)
        act = (up + 1.0) * gate * jax.nn.sigmoid(SWIGLU_ALPHA * gate)
        return act @ w_down[e] + b_down[e]

    yb = lax.map(expert_block, (row_tok.reshape(n_blocks, EXPERT_BLOCK), block_expert))
    y = yb.reshape(n_rows, D) * row_w[:, None].astype(yb.dtype)
    return jax.ops.segment_sum(y, row_tok, num_segments=T + 1)[:T]


def hybrid_layer(x, positions, g_mix, w_in, g_q_latent, w_uq, g_kv_latent, w_ukv,
                 g_q_nope, g_q_rope, g_k_nope, g_k_rope, w_pool, s_pool, w_out,
                 g_ffn, w_router, b_router, w_gate_up, b_gate_up, w_down, b_down):
    B, S, D = x.shape
    h = rms_norm(x, g_mix)
    z = h @ w_in
    y_a = mla_branch(z[..., OFF_Q:OFF_KV], z[..., OFF_KV:OFF_KR], z[..., OFF_KR:OFF_POOL],
                     positions, g_q_latent, w_uq, g_kv_latent, w_ukv,
                     g_q_nope, g_q_rope, g_k_nope, g_k_rope)
    y_p = pool_branch(z[..., OFF_POOL:OFF_GA], w_pool, s_pool)
    merged = jax.nn.sigmoid(z[..., OFF_GA:OFF_GP]) * y_a + jax.nn.sigmoid(z[..., OFF_GP:]) * y_p
    x = x + merged @ w_out
    h2 = rms_norm(x, g_ffn).reshape(B * S, D)
    x = x + moe_ffn(h2, w_router, b_router, w_gate_up, b_gate_up, w_down, b_down).reshape(B, S, D)
    return x


def setup_inputs(seed: int = 0) -> dict:
    key = jax.random.key(seed)
    ks = jax.random.split(key, 24)
    f32 = jnp.float32

    def dense(k, shape, fan_in):
        return jax.random.normal(k, shape, f32) * (fan_in ** -0.5)

    def gain(k, n):
        return 1.0 + 0.1 * jax.random.normal(k, (n,), f32)

    def small(k, shape):
        return 0.01 * jax.random.normal(k, shape, f32)

    return {
        "x": jax.random.normal(ks[0], (BATCH, SEQ, D_MODEL), f32),
        "positions": jnp.broadcast_to(jnp.arange(SEQ, dtype=jnp.int32), (BATCH, SEQ)),
        "g_mix": gain(ks[1], D_MODEL),
        "w_in": dense(ks[2], (D_MODEL, IN_PROJ_WIDTH), D_MODEL),
        "g_q_latent": gain(ks[3], Q_LORA_RANK),
        "w_uq": dense(ks[4], (Q_LORA_RANK, N_HEADS * QK_HEAD_DIM), Q_LORA_RANK),
        "g_kv_latent": gain(ks[5], KV_LORA_RANK),
        "w_ukv": dense(ks[6], (KV_LORA_RANK, N_HEADS * (QK_NOPE_DIM + V_HEAD_DIM)), KV_LORA_RANK),
        "g_q_nope": gain(ks[7], QK_NOPE_DIM),
        "g_q_rope": gain(ks[8], QK_ROPE_DIM),
        "g_k_nope": gain(ks[9], QK_NOPE_DIM),
        "g_k_rope": gain(ks[10], QK_ROPE_DIM),
        "w_pool": dense(ks[11], (POOL_GROUPS, POOL_GROUP_DIM, POOL_OUT_GROUP_DIM), POOL_GROUP_DIM),
        "s_pool": gain(ks[12], D_MODEL),
        "w_out": dense(ks[13], (D_MODEL, D_MODEL), D_MODEL),
        "g_ffn": gain(ks[14], D_MODEL),
        "w_router": dense(ks[15], (D_MODEL, N_EXPERTS), D_MODEL),
        "b_router": small(ks[16], (N_EXPERTS,)),
        "w_gate_up": dense(ks[17], (N_EXPERTS, D_MODEL, 2 * D_FF), D_MODEL),
        "b_gate_up": small(ks[18], (N_EXPERTS, 2 * D_FF)),
        "w_down": dense(ks[19], (N_EXPERTS, D_FF, D_MODEL), D_FF),
        "b_down": small(ks[20], (N_EXPERTS, D_MODEL)),
    }


def reference(x, positions, g_mix, w_in, g_q_latent, w_uq, g_kv_latent, w_ukv,
              g_q_nope, g_q_rope, g_k_nope, g_k_rope, w_pool, s_pool, w_out,
              g_ffn, w_router, b_router, w_gate_up, b_gate_up, w_down, b_down):
    for _ in range(DEPTH):
        x = hybrid_layer(x, positions, g_mix, w_in, g_q_latent, w_uq, g_kv_latent, w_ukv,
                         g_q_nope, g_q_rope, g_k_nope, g_k_rope, w_pool, s_pool, w_out,
                         g_ffn, w_router, b_router, w_gate_up, b_gate_up, w_down, b_down)
    return x
```

```python
import functools
import math
from typing import NamedTuple

import numpy as np
import jax
import jax.numpy as jnp
from jax import lax
from jax.experimental import pallas as pl
from jax.experimental.pallas import tpu as pltpu

F32 = jnp.float32
BF16 = jnp.bfloat16

RMS_EPS = 1e-6
NEG_INF = -1e30
ROPE_THETA = 10000.0
SWIGLU_LIMIT = 7.0
SWIGLU_ALPHA = 1.702
POOL_WINDOWS = (2, 4, 8, 16)
TOP_K = 4

LANES = 128
ROW_SUBLANES = 16
POOL_HALO = 16
VMEM_LIMIT = 56 * 1024 * 1024


class Dims(NamedTuple):
    batch: int = 2
    seq: int = 8192
    d_model: int = 2048
    n_heads: int = 16
    nope: int = 128
    rope: int = 64
    v_dim: int = 128
    q_rank: int = 512
    kv_rank: int = 512
    pool_width: int = 1024
    n_experts: int = 32
    d_ff: int = 2048
    tm_in: int = 1024
    tn_in: int = 512
    tm_qkv: int = 512
    tq: int = 512
    tk: int = 512
    tm_mix: int = 256
    tm_route: int = 512
    tm_disp: int = 256
    tm_moe: int = 512
    tf_moe: int = 512
    tm_comb: int = 256


def _cparams(sem, vmem=VMEM_LIMIT):
    return pltpu.CompilerParams(dimension_semantics=sem, vmem_limit_bytes=vmem)


def _const_spec(shape):
    nd = len(shape)
    return pl.BlockSpec(shape, lambda *_: (0,) * nd, pipeline_mode=pl.Buffered(1))


def _rms(t, gain):
    return t * lax.rsqrt(jnp.mean(t * t, axis=-1, keepdims=True) + RMS_EPS) * gain


def _inproj_kernel(x_ref, g_ref, w_ref, z_ref, h_scr):
    @pl.when(pl.program_id(1) == 0)
    def _():
        h_scr[...] = _rms(x_ref[...], g_ref[...]).astype(BF16)

    z_ref[...] = jnp.dot(h_scr[...], w_ref[...], preferred_element_type=F32).astype(z_ref.dtype)


def _inproj(x2d, g_mix, w_in_r, d: Dims):
    t, dm = x2d.shape
    nz = w_in_r.shape[1]
    tm, tn = d.tm_in, d.tn_in
    return pl.pallas_call(
        _inproj_kernel,
        out_shape=jax.ShapeDtypeStruct((t, nz), BF16),
        grid=(t // tm, nz // tn),
        in_specs=[
            pl.BlockSpec((tm, dm), lambda i, j: (i, 0)),
            pl.BlockSpec((1, dm), lambda i, j: (0, 0)),
            pl.BlockSpec((dm, tn), lambda i, j: (0, j)),
        ],
        out_specs=pl.BlockSpec((tm, tn), lambda i, j: (i, j)),
        scratch_shapes=[pltpu.VMEM((tm, dm), BF16)],
        compiler_params=_cparams(("parallel", "arbitrary")),
        name="inproj",
    )(x2d, g_mix, w_in_r)


def _rope_tile(t, cosf, sinf, half):
    lane = lax.broadcasted_iota(jnp.int32, t.shape, 1)
    swapped = jnp.where(lane < half, pltpu.roll(t, LANES - half, 1), pltpu.roll(t, half, 1))
    return t * cosf + swapped * sinf


def _qkv_kernel(zq_ref, zkv_ref, zkr_ref, pos_ref, invf_ref, sgn_ref,
                gql_ref, gkvl_ref, gqn_ref, gqr_ref, gkn_ref, gkr_ref,
                wq_ref, wk_ref, wv_ref,
                q_ref, k_ref, v_ref, *, n_heads, rope, scale):
    half = rope // 2
    ang = pos_ref[...] * invf_ref[...]
    cosf = jnp.cos(ang)
    sinf = jnp.sin(ang) * sgn_ref[...]

    hq = _rms(zq_ref[...].astype(F32), gql_ref[...]).astype(BF16)
    hkv = _rms(zkv_ref[...].astype(F32), gkvl_ref[...]).astype(BF16)

    kr = zkr_ref[...].astype(F32)
    kr = kr * lax.rsqrt(jnp.sum(kr * kr, axis=-1, keepdims=True) / rope + RMS_EPS) * gkr_ref[...]
    kpe = _rope_tile(kr, cosf, sinf, half).astype(BF16)

    for h in range(n_heads):
        qh = jnp.dot(hq, wq_ref[:, h * 2 * LANES:(h + 1) * 2 * LANES], preferred_element_type=F32)
        qn = _rms(qh[:, :LANES], gqn_ref[...]) * scale
        qp = qh[:, LANES:]
        qp = qp * lax.rsqrt(jnp.sum(qp * qp, axis=-1, keepdims=True) / rope + RMS_EPS) * gqr_ref[...]
        qp = _rope_tile(qp, cosf, sinf, half) * scale
        q_ref[:, h * 2 * LANES:h * 2 * LANES + LANES] = qn.astype(BF16)
        q_ref[:, h * 2 * LANES + LANES:(h + 1) * 2 * LANES] = qp.astype(BF16)

        kh = jnp.dot(hkv, wk_ref[:, h * LANES:(h + 1) * LANES], preferred_element_type=F32)
        k_ref[:, h * 2 * LANES:h * 2 * LANES + LANES] = _rms(kh, gkn_ref[...]).astype(BF16)
        k_ref[:, h * 2 * LANES + LANES:(h + 1) * 2 * LANES] = kpe

        vh = jnp.dot(hkv, wv_ref[:, h * LANES:(h + 1) * LANES], preferred_element_type=F32)
        v_ref[:, h * LANES:(h + 1) * LANES] = vh.astype(BF16)


def _qkv(z, pos_f, invf, sgn, gql, gkvl, gqn, gqr, gkn, gkr, wq, wk, wv, d: Dims, off_kr):
    t = z.shape[0]
    tm = d.tm_qkv
    hq = d.n_heads * 2 * LANES
    hv = d.n_heads * d.v_dim
    row = lambda shape: pl.BlockSpec(shape, lambda i: (0, 0))
    kern = functools.partial(_qkv_kernel, n_heads=d.n_heads, rope=d.rope,
                             scale=1.0 / math.sqrt(d.nope + d.rope))
    return pl.pallas_call(
        kern,
        out_shape=(jax.ShapeDtypeStruct((t, hq), BF16),
                   jax.ShapeDtypeStruct((t, hq), BF16),
                   jax.ShapeDtypeStruct((t, hv), BF16)),
        grid=(t // tm,),
        in_specs=[
            pl.BlockSpec((tm, d.q_rank), lambda i: (i, 0)),
            pl.BlockSpec((tm, d.kv_rank), lambda i: (i, d.q_rank // d.kv_rank)),
            pl.BlockSpec((tm, LANES), lambda i: (i, off_kr // LANES)),
            pl.BlockSpec((tm, 1), lambda i: (i, 0)),
            row((1, LANES)), row((1, LANES)),
            row((1, d.q_rank)), row((1, d.kv_rank)),
            row((1, LANES)), row((1, LANES)), row((1, LANES)), row((1, LANES)),
            _const_spec(wq.shape), _const_spec(wk.shape), _const_spec(wv.shape),
        ],
        out_specs=(pl.BlockSpec((tm, hq), lambda i: (i, 0)),
                   pl.BlockSpec((tm, hq), lambda i: (i, 0)),
                   pl.BlockSpec((tm, hv), lambda i: (i, 0))),
        compiler_params=_cparams(("parallel",)),
        name="qkv",
    )(z, z, z, pos_f, invf, sgn, gql, gkvl, gqn, gqr, gkn, gkr, wq, wk, wv)


def _attn_kernel(q_ref, k_ref, v_ref, o_ref, m_scr, l_scr, acc_scr, *, tq, tk):
    i = pl.program_id(2)
    q = q_ref[...]
    m_scr[...] = jnp.full(m_scr.shape, NEG_INF, F32)
    l_scr[...] = jnp.zeros(l_scr.shape, F32)
    acc_scr[...] = jnp.zeros(acc_scr.shape, F32)

    def step(off, mask):
        k = k_ref[pl.ds(off, tk), :]
        v = v_ref[pl.ds(off, tk), :]
        s = lax.dot_general(q, k, (((1,), (1,)), ((), ())), preferred_element_type=F32)
        if mask is not None:
            s = jnp.where(mask, s, NEG_INF)
        m_old = m_scr[...]
        m_new = jnp.maximum(m_old, jnp.max(s, axis=-1, keepdims=True))
        p = jnp.exp(s - m_new)
        alpha = jnp.exp(m_old - m_new)
        l_scr[...] = alpha * l_scr[...] + jnp.sum(p, axis=-1, keepdims=True)
        acc_scr[...] = alpha * acc_scr[...] + jnp.dot(p.astype(BF16), v, preferred_element_type=F32)
        m_scr[...] = m_new

    n_sub = tq // tk

    def body(c, carry):
        step(pl.multiple_of(c * tk, tk), None)
        return carry

    lax.fori_loop(0, i * n_sub, body, 0)

    row = lax.broadcasted_iota(jnp.int32, (tq, tk), 0)
    col = lax.broadcasted_iota(jnp.int32, (tq, tk), 1)
    for c in range(n_sub):
        off = pl.multiple_of(i * tq + c * tk, tk)
        step(off, (col + c * tk) <= row)

    o_ref[...] = (acc_scr[...] / l_scr[...]).astype(o_ref.dtype)


def _attn(q, k, v, d: Dims):
    b, s, h = d.batch, d.seq, d.n_heads
    tq, tk = d.tq, d.tk
    nq = s // tq
    kern = functools.partial(_attn_kernel, tq=tq, tk=tk)
    return pl.pallas_call(
        kern,
        out_shape=jax.ShapeDtypeStruct((b * s, h * d.v_dim), BF16),
        grid=(b, h, nq),
        in_specs=[
            pl.BlockSpec((tq, 2 * LANES), lambda bi, hi, i: (bi * nq + i, hi)),
            pl.BlockSpec((s, 2 * LANES), lambda bi, hi, i: (bi, hi)),
            pl.BlockSpec((s, d.v_dim), lambda bi, hi, i: (bi, hi)),
        ],
        out_specs=pl.BlockSpec((tq, d.v_dim), lambda bi, hi, i: (bi * nq + i, hi)),
        scratch_shapes=[pltpu.VMEM((tq, 1), F32), pltpu.VMEM((tq, 1), F32),
                        pltpu.VMEM((tq, d.v_dim), F32)],
        compiler_params=_cparams(("parallel", "parallel", "arbitrary")),
        name="attn",
    )(q, k, v)


def _mixout_kernel(u_ref, halo_ref, ga_ref, gp_ref, ya_ref, x_ref,
                   wpool_ref, spool_ref, wout_ref, gffn_ref, wr_ref, br_ref,
                   x1_ref, h3_ref, lg_ref, merged_scr, *, tm, seq, n_groups):
    i = pl.program_id(0)
    row0 = (i * tm) % seq
    gdim = u_ref.shape[1] // n_groups
    odim = ga_ref.shape[1] // n_groups
    pos = row0 + lax.broadcasted_iota(jnp.int32, (tm, 1), 0)
    halo_on = (row0 > 0).astype(F32)

    for g in range(n_groups):
        w = POOL_WINDOWS[g]
        ug = u_ref[:, g * gdim:(g + 1) * gdim].astype(F32)
        hg = halo_ref[:, g * gdim:(g + 1) * gdim].astype(F32) * halo_on
        ext = jnp.concatenate([hg, ug], axis=0)
        shift = 1
        while shift < w:
            ext = ext + pltpu.roll(ext, shift, 0)
            shift *= 2
        wsum = ext[POOL_HALO:, :]
        count = jnp.minimum(pos + 1, w).astype(F32)
        pooled = (wsum / count - ug).astype(BF16)
        yp = jnp.dot(pooled, wpool_ref[g], preferred_element_type=F32)
        yp = yp * spool_ref[:, g * odim:(g + 1) * odim]
        sl = slice(g * odim, (g + 1) * odim)
        merged = (jax.nn.sigmoid(ga_ref[:, sl].astype(F32)) * ya_ref[:, sl].astype(F32)
                  + jax.nn.sigmoid(gp_ref[:, sl].astype(F32)) * yp)
        merged_scr[:, sl] = merged.astype(BF16)

    x1 = x_ref[...] + jnp.dot(merged_scr[...], wout_ref[...], preferred_element_type=F32)
    x1_ref[...] = x1
    hn = _rms(x1, gffn_ref[...])
    lg_ref[...] = jnp.dot(hn, wr_ref[...], preferred_element_type=F32,
                          precision=lax.Precision.HIGHEST) + br_ref[...]
    for s in range(ROW_SUBLANES):
        h3_ref[pl.ds(s, tm, stride=ROW_SUBLANES), :] = hn[:, s * LANES:(s + 1) * LANES]


def _mixout(z, ya, x2d, wpool, spool, wout, gffn, wr, br, d: Dims, off_pool, off_ga, off_gp):
    t, dm = x2d.shape
    tm = d.tm_mix
    pw = d.pool_width
    n_groups = len(POOL_WINDOWS)
    hb = tm // POOL_HALO
    kern = functools.partial(_mixout_kernel, tm=tm, seq=d.seq, n_groups=n_groups)
    return pl.pallas_call(
        kern,
        out_shape=(jax.ShapeDtypeStruct((t, dm), F32),
                   jax.ShapeDtypeStruct((t * ROW_SUBLANES, LANES), F32),
                   jax.ShapeDtypeStruct((t, LANES), F32)),
        grid=(t // tm,),
        in_specs=[
            pl.BlockSpec((tm, pw), lambda i: (i, off_pool // pw)),
            pl.BlockSpec((POOL_HALO, pw), lambda i: (jnp.maximum(i * hb - 1, 0), off_pool // pw)),
            pl.BlockSpec((tm, dm), lambda i: (i, off_ga // dm)),
            pl.BlockSpec((tm, dm), lambda i: (i, off_gp // dm)),
            pl.BlockSpec((tm, dm), lambda i: (i, 0)),
            pl.BlockSpec((tm, dm), lambda i: (i, 0)),
            _const_spec(wpool.shape), _const_spec(spool.shape), _const_spec(wout.shape),
            _const_spec(gffn.shape), _const_spec(wr.shape), _const_spec(br.shape),
        ],
        out_specs=(pl.BlockSpec((tm, dm), lambda i: (i, 0)),
                   pl.BlockSpec((tm * ROW_SUBLANES, LANES), lambda i: (i, 0)),
                   pl.BlockSpec((tm, LANES), lambda i: (i, 0))),
        scratch_shapes=[pltpu.VMEM((tm, dm), BF16)],
        compiler_params=_cparams(("parallel",)),
        name="mixout",
    )(z, z, z, z, ya, x2d, wpool, spool, wout, gffn, wr, br)


def _lane_excl_cumsum(v):
    lane = lax.broadcasted_iota(jnp.int32, v.shape, 1)
    inc = v
    shift = 1
    while shift < LANES:
        inc = inc + jnp.where(lane >= shift, pltpu.roll(inc, shift, 1), 0.0)
        shift *= 2
    return inc - v


def _route_kernel(lg_ref, dest_ref, gate_ref, cnt_ref, carry_scr, base_scr, *, tm, n_experts, tile):
    phase = pl.program_id(0)
    i = pl.program_id(1)
    lane = lax.broadcasted_iota(jnp.int32, (tm, LANES), 1)
    work = jnp.where(lane < n_experts, lg_ref[...], -jnp.inf)

    vals, hots = [], []
    for _ in range(TOP_K):
        m = jnp.max(work, axis=-1, keepdims=True)
        first = jnp.min(jnp.where(work == m, lane, LANES), axis=-1, keepdims=True)
        hot = lane == first
        vals.append(m)
        hots.append(hot)
        work = jnp.where(hot, -jnp.inf, work)
    chosen = jnp.zeros((tm, LANES), F32)
    for hot in hots:
        chosen = chosen + hot.astype(F32)

    @pl.when((phase == 0) & (i == 0))
    def _():
        carry_scr[...] = jnp.zeros(carry_scr.shape, F32)

    @pl.when((phase == 1) & (i == 0))
    def _():
        counts = carry_scr[...]
        padded = jnp.ceil(counts / tile) * tile
        base_scr[...] = _lane_excl_cumsum(padded)
        cnt_ref[...] = counts
        carry_scr[...] = jnp.zeros(carry_scr.shape, F32)

    @pl.when(phase == 1)
    def _():
        r = lax.broadcasted_iota(jnp.int32, (tm, tm), 0)
        c = lax.broadcasted_iota(jnp.int32, (tm, tm), 1)
        tri = (c < r).astype(BF16)
        before = jnp.dot(tri, chosen.astype(BF16), preferred_element_type=F32)
        slot = before + carry_scr[...] + base_scr[...]
        exps = [jnp.exp(v - vals[0]) for v in vals]
        den = exps[0] + exps[1] + exps[2] + exps[3]
        for k in range(TOP_K):
            dk = jnp.sum(jnp.where(hots[k], slot, 0.0), axis=-1, keepdims=True)
            dest_ref[:, k:k + 1] = dk.astype(jnp.int32)
            gate_ref[:, k:k + 1] = exps[k] / den

    carry_scr[...] = carry_scr[...] + jnp.sum(chosen, axis=0, keepdims=True)


def _route(logits, d: Dims):
    t = logits.shape[0]
    tm = d.tm_route
    kern = functools.partial(_route_kernel, tm=tm, n_experts=d.n_experts, tile=d.tm_moe)
    return pl.pallas_call(
        kern,
        out_shape=(jax.ShapeDtypeStruct((t, TOP_K), jnp.int32),
                   jax.ShapeDtypeStruct((t, TOP_K), F32),
                   jax.ShapeDtypeStruct((1, LANES), F32)),
        grid=(2, t // tm),
        in_specs=[pl.BlockSpec((tm, LANES), lambda p, i: (i, 0))],
        out_specs=(pl.BlockSpec((tm, TOP_K), lambda p, i: (i * p, 0)),
                   pl.BlockSpec((tm, TOP_K), lambda p, i: (i * p, 0)),
                   pl.BlockSpec((1, LANES), lambda p, i: (0, 0))),
        scratch_shapes=[pltpu.VMEM((1, LANES), F32), pltpu.VMEM((1, LANES), F32)],
        compiler_params=_cparams(("arbitrary", "arbitrary")),
        name="route",
    )(logits)


def _row_copy(src_ref, src_row, dst_ref, dst_row, sem):
    rs = ROW_SUBLANES
    return pltpu.make_async_copy(
        src_ref.at[pl.ds(pl.multiple_of(src_row * rs, rs), rs)],
        dst_ref.at[pl.ds(pl.multiple_of(dst_row * rs, rs), rs)], sem)


def _dispatch_kernel(dest_ref, h3_ref, xs_ref, sem, *, tm):
    i = pl.program_id(0)

    def issue(r, carry):
        for k in range(TOP_K):
            _row_copy(h3_ref, r, xs_ref, dest_ref[(i * tm + r) * TOP_K + k], sem).start()
        return carry

    lax.fori_loop(0, tm, issue, 0)

    def drain(r, carry):
        for k in range(TOP_K):
            _row_copy(h3_ref, r, xs_ref, dest_ref[(i * tm + r) * TOP_K + k], sem).wait()
        return carry

    lax.fori_loop(0, tm, drain, 0)


def _dispatch(dest_flat, h3, n_rows, d: Dims):
    t = h3.shape[0] // ROW_SUBLANES
    tm = d.tm_disp
    kern = functools.partial(_dispatch_kernel, tm=tm)
    return pl.pallas_call(
        kern,
        out_shape=jax.ShapeDtypeStruct((n_rows * ROW_SUBLANES, LANES), F32),
        grid_spec=pltpu.PrefetchScalarGridSpec(
            num_scalar_prefetch=1,
            grid=(t // tm,),
            in_specs=[pl.BlockSpec((tm * ROW_SUBLANES, LANES), lambda i, dest: (i, 0))],
            out_specs=pl.BlockSpec(memory_space=pl.ANY),
            scratch_shapes=[pltpu.SemaphoreType.DMA],
        ),
        compiler_params=_cparams(("arbitrary",)),
        name="dispatch",
    )(dest_flat, h3)


def _moe_kernel(te_ref, ts_ref, tv_ref, nu_ref,
                xs_ref, wg_ref, wu_ref, bg_ref, bu_ref, wd_ref, bd_ref,
                y_ref, xb_scr, acc_scr, *, tm, nj):
    i = pl.program_id(0)
    j = pl.program_id(1)

    @pl.when(i < nu_ref[0])
    def _():
        @pl.when(j == 0)
        def _():
            valid = lax.broadcasted_iota(jnp.int32, (tm, LANES), 0) < tv_ref[i]
            for s in range(ROW_SUBLANES):
                xs = xs_ref[pl.ds(s, tm, stride=ROW_SUBLANES), :]
                xb_scr[:, s * LANES:(s + 1) * LANES] = jnp.where(valid, xs, 0.0).astype(BF16)

        x = xb_scr[...]
        gate = jnp.dot(x, wg_ref[0], preferred_element_type=F32) + bg_ref[0]
        up = jnp.dot(x, wu_ref[0], preferred_element_type=F32) + bu_ref[0]
        gate = jnp.minimum(gate, SWIGLU_LIMIT)
        up = jnp.clip(up, -SWIGLU_LIMIT, SWIGLU_LIMIT)
        act = (up + 1.0) * gate * jax.nn.sigmoid(SWIGLU_ALPHA * gate)
        part = jnp.dot(act.astype(BF16), wd_ref[0], preferred_element_type=F32)

        @pl.when(j == 0)
        def _():
            acc_scr[...] = part + bd_ref[0]

        @pl.when(j > 0)
        def _():
            acc_scr[...] = acc_scr[...] + part

        @pl.when(j == nj - 1)
        def _():
            for s in range(ROW_SUBLANES):
                y_ref[pl.ds(s, tm, stride=ROW_SUBLANES), :] = acc_scr[:, s * LANES:(s + 1) * LANES]


def _moe(tile_expert, tile_src, tile_valid, n_used, xs3, wgu, bgu, wd, bd, d: Dims):
    n_rows = xs3.shape[0] // ROW_SUBLANES
    tm, tf = d.tm_moe, d.tf_moe
    dm, ff = d.d_model, d.d_ff
    nj = ff // tf
    n_tiles = n_rows // tm

    def jj(i, j, nu):
        return jnp.where(i < nu[0], j, nj - 1)

    kern = functools.partial(_moe_kernel, tm=tm, nj=nj)
    return pl.pallas_call(
        kern,
        out_shape=jax.ShapeDtypeStruct((n_rows * ROW_SUBLANES, LANES), F32),
        grid_spec=pltpu.PrefetchScalarGridSpec(
            num_scalar_prefetch=4,
            grid=(n_tiles, nj),
            in_specs=[
                pl.BlockSpec((tm * ROW_SUBLANES, LANES), lambda i, j, te, ts, tv, nu: (ts[i], 0)),
                pl.BlockSpec((1, dm, tf), lambda i, j, te, ts, tv, nu: (te[i], 0, jj(i, j, nu))),
                pl.BlockSpec((1, dm, tf), lambda i, j, te, ts, tv, nu: (te[i], 0, nj + jj(i, j, nu))),
                pl.BlockSpec((1, 1, tf), lambda i, j, te, ts, tv, nu: (te[i], 0, jj(i, j, nu))),
                pl.BlockSpec((1, 1, tf), lambda i, j, te, ts, tv, nu: (te[i], 0, nj + jj(i, j, nu))),
                pl.BlockSpec((1, tf, dm), lambda i, j, te, ts, tv, nu: (te[i], jj(i, j, nu), 0)),
                pl.BlockSpec((1, 1, dm), lambda i, j, te, ts, tv, nu: (te[i], 0, 0)),
            ],
            out_specs=pl.BlockSpec((tm * ROW_SUBLANES, LANES), lambda i, j, te, ts, tv, nu: (ts[i], 0)),
            scratch_shapes=[pltpu.VMEM((tm, dm), BF16), pltpu.VMEM((tm, dm), F32)],
        ),
        compiler_params=_cparams(("arbitrary", "arbitrary")),
        name="moe",
    )(tile_expert, tile_src, tile_valid, n_used, xs3, wgu, wgu, bgu, bgu, wd, bd)


def _combine_kernel(dest_ref, y3_ref, gate_ref, x1_ref, o_ref, ybuf, sem, *, tm):
    i = pl.program_id(0)

    def issue(r, carry):
        for k in range(TOP_K):
            _row_copy(y3_ref, dest_ref[(i * tm + r) * TOP_K + k], ybuf, k * tm + r, sem).start()
        return carry

    lax.fori_loop(0, tm, issue, 0)

    def drain(r, carry):
        for k in range(TOP_K):
            _row_copy(y3_ref, dest_ref[(i * tm + r) * TOP_K + k], ybuf, k * tm + r, sem).wait()
        return carry

    lax.fori_loop(0, tm, drain, 0)

    gates = gate_ref[...]
    for s in range(ROW_SUBLANES):
        acc = x1_ref[:, s * LANES:(s + 1) * LANES]
        for k in range(TOP_K):
            yk = ybuf[pl.ds(k * tm * ROW_SUBLANES + s, tm, stride=ROW_SUBLANES), :]
            acc = acc + gates[:, k:k + 1] * yk
        o_ref[:, s * LANES:(s + 1) * LANES] = acc


def _combine(dest_flat, y3, gates, x1, d: Dims):
    t, dm = x1.shape
    tm = d.tm_comb
    kern = functools.partial(_combine_kernel, tm=tm)
    return pl.pallas_call(
        kern,
        out_shape=jax.ShapeDtypeStruct((t, dm), F32),
        grid_spec=pltpu.PrefetchScalarGridSpec(
            num_scalar_prefetch=1,
            grid=(t // tm,),
            in_specs=[
                pl.BlockSpec(memory_space=pl.ANY),
                pl.BlockSpec((tm, TOP_K), lambda i, dest: (i, 0)),
                pl.BlockSpec((tm, dm), lambda i, dest: (i, 0)),
            ],
            out_specs=pl.BlockSpec((tm, dm), lambda i, dest: (i, 0)),
            scratch_shapes=[pltpu.VMEM((TOP_K * tm * ROW_SUBLANES, LANES), F32),
                            pltpu.SemaphoreType.DMA],
        ),
        compiler_params=_cparams(("arbitrary",)),
        name="combine",
    )(dest_flat, y3, gates, x1)


def _pad_lanes(v, fill=0.0):
    v = v.reshape(1, -1).astype(F32)
    return jnp.pad(v, ((0, 0), (0, LANES - v.shape[1])), constant_values=fill)


def _layer(d: Dims, x, positions, g_mix, w_in, g_q_latent, w_uq, g_kv_latent, w_ukv,
           g_q_nope, g_q_rope, g_k_nope, g_k_rope, w_pool, s_pool, w_out,
           g_ffn, w_router, b_router, w_gate_up, b_gate_up, w_down, b_down):
    b, s, dm, h = d.batch, d.seq, d.d_model, d.n_heads
    t = b * s
    assert d.nope == LANES and d.v_dim == LANES and d.rope <= LANES and dm == ROW_SUBLANES * LANES
    assert d.q_rank == d.kv_rank and d.pool_width % d.q_rank == 0 and dm % d.pool_width == 0

    o_q, o_kv = 0, d.q_rank
    o_kr = o_kv + d.kv_rank
    o_pool = o_kr + d.rope
    o_ga = o_pool + d.pool_width
    o_gp = o_ga + dm
    n_pool = -(-(d.q_rank + d.kv_rank) // d.pool_width) * d.pool_width
    n_ga = -(-(n_pool + d.pool_width) // dm) * dm
    n_gp = n_ga + dm
    n_kr = n_gp + dm
    nz = -(-(n_kr + LANES) // d.tn_in) * d.tn_in
    zeros = lambda n: jnp.zeros((dm, n), w_in.dtype)
    w_in_r = jnp.concatenate([
        w_in[:, o_q:o_kr], zeros(n_pool - (d.q_rank + d.kv_rank)),
        w_in[:, o_pool:o_ga], zeros(n_ga - n_pool - d.pool_width),
        w_in[:, o_ga:o_gp], w_in[:, o_gp:],
        w_in[:, o_kr:o_pool], zeros(nz - n_kr - d.rope)], axis=1).astype(BF16)

    qk = d.nope + d.rope
    wq3 = w_uq.reshape(d.q_rank, h, qk)
    wq = jnp.concatenate([wq3, jnp.zeros((d.q_rank, h, 2 * LANES - qk), w_uq.dtype)], axis=2)
    wq = wq.reshape(d.q_rank, h * 2 * LANES).astype(BF16)
    wkv3 = w_ukv.reshape(d.kv_rank, h, d.nope + d.v_dim)
    wk = wkv3[:, :, :d.nope].reshape(d.kv_rank, h * d.nope).astype(BF16)
    wv = wkv3[:, :, d.nope:].reshape(d.kv_rank, h * d.v_dim).astype(BF16)

    half = d.rope // 2
    inv_freq = ROPE_THETA ** (-np.arange(0, d.rope, 2, dtype=np.float32) / d.rope)
    invf = np.zeros((1, LANES), np.float32)
    invf[0, :half] = inv_freq
    invf[0, half:d.rope] = inv_freq
    sgn = np.zeros((1, LANES), np.float32)
    sgn[0, :half] = -1.0
    sgn[0, half:d.rope] = 1.0

    x2d = x.reshape(t, dm)
    pos_f = positions.reshape(t, 1).astype(F32)

    z = _inproj(x2d, g_mix.reshape(1, dm), w_in_r, d)
    q, k, v = _qkv(z, pos_f, jnp.asarray(invf), jnp.asarray(sgn),
                   g_q_latent.reshape(1, -1), g_kv_latent.reshape(1, -1),
                   _pad_lanes(g_q_nope), _pad_lanes(g_q_rope), _pad_lanes(g_k_nope), _pad_lanes(g_k_rope),
                   wq, wk, wv, d, n_kr)
    ya = _attn(q, k, v, d)
    wr = jnp.pad(w_router.astype(F32), ((0, 0), (0, LANES - d.n_experts)))
    x1, h3, logits = _mixout(z, ya, x2d, w_pool.astype(BF16), s_pool.reshape(1, dm), w_out.astype(BF16),
                             g_ffn.reshape(1, dm), wr, _pad_lanes(b_router), d, n_pool, n_ga, n_gp)

    dest, gates, counts = _route(logits, d)
    tm = d.tm_moe
    n_rows = t * TOP_K + d.n_experts * tm
    n_tiles = n_rows // tm
    cnt = counts[0, :d.n_experts].astype(jnp.int32)
    ends = jnp.cumsum((cnt + tm - 1) // tm * tm)
    n_used = ends[-1] // tm
    tile_ids = jnp.minimum(jnp.arange(n_tiles, dtype=jnp.int32), n_used - 1)
    tile_expert = jnp.sum(ends[None, :] <= (tile_ids * tm)[:, None], axis=1).astype(jnp.int32)
    tile_expert = jnp.minimum(tile_expert, d.n_experts - 1)
    starts = ends - (cnt + tm - 1) // tm * tm
    tile_valid = jnp.clip(starts[tile_expert] + cnt[tile_expert] - tile_ids * tm, 0, tm).astype(jnp.int32)

    dest_flat = dest.reshape(t * TOP_K)
    xs3 = _dispatch(dest_flat, h3, n_rows, d)
    y3 = _moe(tile_expert, tile_ids, tile_valid, n_used.reshape(1).astype(jnp.int32), xs3,
              w_gate_up.astype(BF16), b_gate_up.reshape(d.n_experts, 1, 2 * d.d_ff),
              w_down.astype(BF16), b_down.reshape(d.n_experts, 1, dm), d)
    out = _combine(dest_flat, y3, gates, x1, d)
    return out.reshape(b, s, dm)


def kernel(x, positions, g_mix, w_in, g_q_latent, w_uq, g_kv_latent, w_ukv, g_q_nope, g_q_rope, g_k_nope, g_k_rope, w_pool, s_pool, w_out, g_ffn, w_router, b_router, w_gate_up, b_gate_up, w_down, b_down):
    return _layer(Dims(), x, positions, g_mix, w_in, g_q_latent, w_uq, g_kv_latent, w_ukv,
                  g_q_nope, g_q_rope, g_k_nope, g_k_rope, w_pool, s_pool, w_out,
                  g_ffn, w_router, b_router, w_gate_up, b_gate_up, w_down, b_down)
```

```python
import functools
import math
from typing import NamedTuple

import numpy as np
import jax
import jax.numpy as jnp
from jax import lax
from jax.experimental import pallas as pl
from jax.experimental.pallas import tpu as pltpu

F32 = jnp.float32
BF16 = jnp.bfloat16

RMS_EPS = 1e-6
NEG_INF = -1e30
ROPE_THETA = 10000.0
SWIGLU_LIMIT = 7.0
SWIGLU_ALPHA = 1.702
POOL_WINDOWS = (2, 4, 8, 16)
TOP_K = 4

LANES = 128
SUBLANES = 8
POOL_HALO = 16
VMEM_LIMIT = 56 * 1024 * 1024


class Dims(NamedTuple):
    batch: int = 2
    seq: int = 8192
    d_model: int = 2048
    n_heads: int = 16
    nope: int = 128
    rope: int = 64
    v_dim: int = 128
    q_rank: int = 512
    kv_rank: int = 512
    pool_width: int = 1024
    n_experts: int = 32
    d_ff: int = 2048
    tm_in: int = 1024
    tn_in: int = 1664
    tq: int = 2048
    tk: int = 512
    attn_gw: int = 512
    tm_mix: int = 256
    tm_route: int = 512
    tm_disp: int = 128
    tm_moe: int = 512
    tf_moe: int = 512
    tm_comb: int = 128


def _cparams(sem, vmem=VMEM_LIMIT):
    return pltpu.CompilerParams(dimension_semantics=sem, vmem_limit_bytes=vmem)


def _const_spec(shape):
    nd = len(shape)
    return pl.BlockSpec(shape, lambda *_: (0,) * nd, pipeline_mode=pl.Buffered(1))


def _rms(t, gain):
    return t * lax.rsqrt(jnp.mean(t * t, axis=-1, keepdims=True) + RMS_EPS) * gain


def _sigmoid(t):
    return 0.5 * jnp.tanh(0.5 * t) + 0.5


def _row_tiles(v):
    return v.reshape(v.shape[0] // SUBLANES, SUBLANES, LANES)


def _inproj_kernel(x_ref, g_ref, w_ref, z_ref, h_scr):
    @pl.when(pl.program_id(1) == 0)
    def _():
        h_scr[...] = _rms(x_ref[...], g_ref[...]).astype(BF16)

    z_ref[...] = jnp.dot(h_scr[...], w_ref[...], preferred_element_type=F32).astype(z_ref.dtype)


def _inproj(x2d, g_mix, w_in_r, d: Dims):
    t, dm = x2d.shape
    nz = w_in_r.shape[1]
    tm, tn = d.tm_in, d.tn_in
    return pl.pallas_call(
        _inproj_kernel,
        out_shape=jax.ShapeDtypeStruct((t, nz), BF16),
        grid=(t // tm, nz // tn),
        in_specs=[
            pl.BlockSpec((tm, dm), lambda i, j: (i, 0)),
            pl.BlockSpec((1, dm), lambda i, j: (0, 0)),
            pl.BlockSpec((dm, tn), lambda i, j: (0, j)),
        ],
        out_specs=pl.BlockSpec((tm, tn), lambda i, j: (i, j)),
        scratch_shapes=[pltpu.VMEM((tm, dm), BF16)],
        compiler_params=_cparams(("parallel", "arbitrary")),
        name="inproj",
    )(x2d, g_mix, w_in_r)


def _rope_tile(t, cosf, sinf, half):
    lane = lax.broadcasted_iota(jnp.int32, t.shape, 1)
    swapped = jnp.where(lane < half, pltpu.roll(t, LANES - half, 1), pltpu.roll(t, half, 1))
    return t * cosf + swapped * sinf


def _qkv_kernel(zq_ref, zkv_ref, zkr_ref, pos_ref, posr_ref, invf_ref, sgn_ref, invft_ref,
                gql_ref, gkvl_ref, gqnt_ref, gqr1_ref, gqr2_ref, gkn_ref, gkr_ref,
                wqt_ref, wk_ref, wvt_ref,
                qt_ref, k_ref, vt_ref, *, n_heads, rope, qscale):
    half = rope // 2
    hw = 2 * LANES
    ang = pos_ref[...] * invf_ref[...]
    cosf = jnp.cos(ang)
    sinf = jnp.sin(ang) * sgn_ref[...]
    angt = invft_ref[...] * posr_ref[...]
    cost = jnp.cos(angt)
    sint = jnp.sin(angt)

    hq = _rms(zq_ref[...].astype(F32), gql_ref[...])
    hkv = _rms(zkv_ref[...].astype(F32), gkvl_ref[...])
    hqt = hq.T.astype(BF16)
    hkvt = hkv.T.astype(BF16)
    hkv = hkv.astype(BF16)

    kr = zkr_ref[...].astype(F32)
    kr = kr * lax.rsqrt(jnp.sum(kr * kr, axis=-1, keepdims=True) / rope + RMS_EPS) * gkr_ref[...]
    kpe = _rope_tile(kr, cosf, sinf, half).astype(BF16)

    for h in range(n_heads):
        qh = jnp.dot(wqt_ref[h * hw:(h + 1) * hw, :], hqt, preferred_element_type=F32)
        qn = qh[:LANES]
        qn = qn * lax.rsqrt(jnp.mean(qn * qn, axis=0, keepdims=True) + RMS_EPS) * gqnt_ref[...]
        t1 = qh[LANES:LANES + half]
        t2 = qh[LANES + half:LANES + rope]
        ss = jnp.sum(t1 * t1, axis=0, keepdims=True) + jnp.sum(t2 * t2, axis=0, keepdims=True)
        r = lax.rsqrt(ss / rope + RMS_EPS)
        t1 = t1 * r * gqr1_ref[...]
        t2 = t2 * r * gqr2_ref[...]
        qt_ref[h * hw:h * hw + LANES, :] = (qn * qscale).astype(BF16)
        qt_ref[h * hw + LANES:h * hw + LANES + half, :] = ((t1 * cost - t2 * sint) * qscale).astype(BF16)
        qt_ref[h * hw + LANES + half:h * hw + LANES + rope, :] = ((t2 * cost + t1 * sint) * qscale).astype(BF16)
        qt_ref[h * hw + LANES + rope:(h + 1) * hw, :] = jnp.zeros((LANES - rope, qt_ref.shape[1]), BF16)

        kh = jnp.dot(hkv, wk_ref[:, h * LANES:(h + 1) * LANES], preferred_element_type=F32)
        k_ref[:, h * hw:h * hw + LANES] = _rms(kh, gkn_ref[...]).astype(BF16)
        k_ref[:, h * hw + LANES:(h + 1) * hw] = kpe

    for c in range(wvt_ref.shape[0] // hw):
        vt = jnp.dot(wvt_ref[c * hw:(c + 1) * hw, :], hkvt, preferred_element_type=F32)
        vt_ref[0, c * hw:(c + 1) * hw, :] = vt.astype(BF16)


def _qkv(z, pos_col, pos_row, consts, gql, gkvl, gqnt, gqr1, gqr2, gkn, gkr, wqt, wk, wvt, d: Dims, off_kr):
    t = z.shape[0]
    tm = d.tk
    invf, sgn, invft = consts
    hq = d.n_heads * 2 * LANES
    hv = d.n_heads * d.v_dim
    row = lambda a: pl.BlockSpec(a.shape, lambda i: (0, 0))
    kern = functools.partial(_qkv_kernel, n_heads=d.n_heads, rope=d.rope,
                             qscale=math.log2(math.e) / math.sqrt(d.nope + d.rope))
    return pl.pallas_call(
        kern,
        out_shape=(jax.ShapeDtypeStruct((hq, t), BF16),
                   jax.ShapeDtypeStruct((t, hq), BF16),
                   jax.ShapeDtypeStruct((t // tm, hv, tm), BF16)),
        grid=(t // tm,),
        in_specs=[
            pl.BlockSpec((tm, d.q_rank), lambda i: (i, 0)),
            pl.BlockSpec((tm, d.kv_rank), lambda i: (i, d.q_rank // d.kv_rank)),
            pl.BlockSpec((tm, LANES), lambda i: (i, off_kr // LANES)),
            pl.BlockSpec((tm, 1), lambda i: (i, 0)),
            pl.BlockSpec((1, tm), lambda i: (0, i)),
            row(invf), row(sgn), row(invft),
            row(gql), row(gkvl), row(gqnt), row(gqr1), row(gqr2), row(gkn), row(gkr),
            _const_spec(wqt.shape), _const_spec(wk.shape), _const_spec(wvt.shape),
        ],
        out_specs=(pl.BlockSpec((hq, tm), lambda i: (0, i)),
                   pl.BlockSpec((tm, hq), lambda i: (i, 0)),
                   pl.BlockSpec((1, hv, tm), lambda i: (i, 0, 0))),
        compiler_params=_cparams(("parallel",)),
        name="qkv",
    )(z, z, z, pos_col, pos_row, invf, sgn, invft, gql, gkvl, gqnt, gqr1, gqr2, gkn, gkr, wqt, wk, wvt)


def _attn_kernel(qt_ref, k_ref, vt_ref, o_ref, *scr, tq, tk, gw):
    i = pl.program_id(2)
    n_sub = tq // tk
    ng = tq // gw
    m_scr, l_scr, acc_scr = scr[:ng], scr[ng:2 * ng], scr[2 * ng:]
    for g in range(ng):
        m_scr[g][...] = jnp.full(m_scr[g].shape, NEG_INF, F32)
        l_scr[g][...] = jnp.zeros(l_scr[g].shape, F32)
        acc_scr[g][...] = jnp.zeros(acc_scr[g].shape, F32)

    def group_step(c, g, mask):
        kc = k_ref[pl.ds(pl.multiple_of(c * tk, tk), tk), :]
        s = jnp.dot(kc, qt_ref[:, g * gw:(g + 1) * gw], preferred_element_type=F32)
        if mask is not None:
            s = jnp.where(mask, s, NEG_INF)
        m_old = m_scr[g][...]
        m_new = jnp.maximum(m_old, jnp.max(s, axis=0, keepdims=True))
        p = jnp.exp2(s - m_new)
        alpha = jnp.exp2(m_old - m_new)
        l_scr[g][...] = alpha * l_scr[g][...] + jnp.sum(p, axis=0, keepdims=True)
        pv = jnp.dot(vt_ref[c], p.astype(BF16), preferred_element_type=F32)
        acc_scr[g][...] = alpha * acc_scr[g][...] + pv
        m_scr[g][...] = m_new

    def body(c, carry):
        for g in range(ng):
            group_step(c, g, None)
        return carry

    lax.fori_loop(0, i * n_sub, body, 0)

    key = lax.broadcasted_iota(jnp.int32, (tk, gw), 0)
    qry = lax.broadcasted_iota(jnp.int32, (tk, gw), 1)
    for cc in range(n_sub):
        for g in range(ng):
            k_lo, k_hi = cc * tk, (cc + 1) * tk - 1
            q_lo, q_hi = g * gw, (g + 1) * gw - 1
            if k_lo > q_hi:
                continue
            mask = None if k_hi <= q_lo else (key + k_lo) <= (qry + q_lo)
            group_step(i * n_sub + cc, g, mask)

    for g in range(ng):
        o = acc_scr[g][...] / l_scr[g][...]
        o_ref[g * gw:(g + 1) * gw, :] = o.T.astype(o_ref.dtype)


def _attn(qt, k, vt, d: Dims):
    b, s, h = d.batch, d.seq, d.n_heads
    tq, tk = d.tq, d.tk
    gw = min(d.attn_gw, tq)
    ng = tq // gw
    nq = s // tq
    kern = functools.partial(_attn_kernel, tq=tq, tk=tk, gw=gw)
    return pl.pallas_call(
        kern,
        out_shape=jax.ShapeDtypeStruct((b * s, h * d.v_dim), BF16),
        grid=(b, h, nq),
        in_specs=[
            pl.BlockSpec((2 * LANES, tq), lambda bi, hi, i: (hi, bi * nq + i)),
            pl.BlockSpec((s, 2 * LANES), lambda bi, hi, i: (bi, hi)),
            pl.BlockSpec((s // tk, d.v_dim, tk), lambda bi, hi, i: (bi, hi, 0)),
        ],
        out_specs=pl.BlockSpec((tq, d.v_dim), lambda bi, hi, i: (bi * nq + i, hi)),
        scratch_shapes=([pltpu.VMEM((1, gw), F32)] * (2 * ng) + [pltpu.VMEM((d.v_dim, gw), F32)] * ng),
        compiler_params=_cparams(("parallel", "parallel", "arbitrary")),
        name="attn",
    )(qt, k, vt)


def _mixout_kernel(u_ref, halo_ref, ga_ref, gp_ref, ya_ref, x_ref,
                   wpool_ref, spool_ref, wout_ref, gffn_ref, wr_ref, br_ref,
                   x1_ref, h4_ref, lg_ref, merged_scr, *, tm, seq, n_groups):
    i = pl.program_id(0)
    row0 = (i * tm) % seq
    gdim = u_ref.shape[1] // n_groups
    odim = ga_ref.shape[1] // n_groups
    pos = row0 + lax.broadcasted_iota(jnp.int32, (tm, 1), 0)
    halo_on = (row0 > 0).astype(F32)

    for g in range(n_groups):
        w = POOL_WINDOWS[g]
        ug = u_ref[:, g * gdim:(g + 1) * gdim].astype(F32)
        hg = halo_ref[:, g * gdim:(g + 1) * gdim].astype(F32) * halo_on
        ext = jnp.concatenate([hg, ug], axis=0)
        shift = 1
        while shift < w:
            ext = ext + pltpu.roll(ext, shift, 0)
            shift *= 2
        wsum = ext[POOL_HALO:, :]
        count = jnp.minimum(pos + 1, w).astype(F32)
        pooled = (wsum / count - ug).astype(BF16)
        yp = jnp.dot(pooled, wpool_ref[g], preferred_element_type=F32)
        yp = yp * spool_ref[:, g * odim:(g + 1) * odim]
        sl = slice(g * odim, (g + 1) * odim)
        merged = (_sigmoid(ga_ref[:, sl].astype(F32)) * ya_ref[:, sl].astype(F32)
                  + _sigmoid(gp_ref[:, sl].astype(F32)) * yp)
        merged_scr[:, sl] = merged.astype(BF16)

    x1 = x_ref[...] + jnp.dot(merged_scr[...], wout_ref[...], preferred_element_type=F32)
    x1_ref[...] = x1
    hn = _rms(x1, gffn_ref[...])
    hi = hn.astype(BF16)
    lo = (hn - hi.astype(F32)).astype(BF16)
    r = jnp.dot(jnp.concatenate([hi, lo], axis=0), wr_ref[...], preferred_element_type=F32)
    lg_ref[...] = (r[:tm, :LANES] + r[:tm, LANES:]) + (r[tm:, :LANES] + r[tm:, LANES:]) + br_ref[...]
    for c in range(h4_ref.shape[1]):
        h4_ref[:, c] = _row_tiles(hn[:, c * LANES:(c + 1) * LANES])


def _mixout(z, ya, x2d, wpool, spool, wout, gffn, wr, br, d: Dims, off_pool, off_ga, off_gp):
    t, dm = x2d.shape
    tm = d.tm_mix
    pw = d.pool_width
    n_groups = len(POOL_WINDOWS)
    hb = tm // POOL_HALO
    nc = dm // LANES
    kern = functools.partial(_mixout_kernel, tm=tm, seq=d.seq, n_groups=n_groups)
    return pl.pallas_call(
        kern,
        out_shape=(jax.ShapeDtypeStruct((t, dm), F32),
                   jax.ShapeDtypeStruct((t // SUBLANES, nc, SUBLANES, LANES), F32),
                   jax.ShapeDtypeStruct((t, LANES), F32)),
        grid=(t // tm,),
        in_specs=[
            pl.BlockSpec((tm, pw), lambda i: (i, off_pool // pw)),
            pl.BlockSpec((POOL_HALO, pw), lambda i: (jnp.maximum(i * hb - 1, 0), off_pool // pw)),
            pl.BlockSpec((tm, dm), lambda i: (i, off_ga // dm)),
            pl.BlockSpec((tm, dm), lambda i: (i, off_gp // dm)),
            pl.BlockSpec((tm, dm), lambda i: (i, 0)),
            pl.BlockSpec((tm, dm), lambda i: (i, 0)),
            _const_spec(wpool.shape), _const_spec(spool.shape), _const_spec(wout.shape),
            _const_spec(gffn.shape), _const_spec(wr.shape), _const_spec(br.shape),
        ],
        out_specs=(pl.BlockSpec((tm, dm), lambda i: (i, 0)),
                   pl.BlockSpec((tm // SUBLANES, nc, SUBLANES, LANES), lambda i: (i, 0, 0, 0)),
                   pl.BlockSpec((tm, LANES), lambda i: (i, 0))),
        scratch_shapes=[pltpu.VMEM((tm, dm), BF16)],
        compiler_params=_cparams(("parallel",)),
        name="mixout",
    )(z, z, z, z, ya, x2d, wpool, spool, wout, gffn, wr, br)


def _lane_excl_cumsum(v):
    lane = lax.broadcasted_iota(jnp.int32, v.shape, 1)
    inc = v
    shift = 1
    while shift < LANES:
        inc = inc + jnp.where(lane >= shift, pltpu.roll(inc, shift, 1), 0.0)
        shift *= 2
    return inc - v


def _route_kernel(lg_ref, dest_ref, gate_ref, cnt_ref, carry_scr, base_scr, *, tm, n_experts, tile):
    phase = pl.program_id(0)
    i = pl.program_id(1)
    lane = lax.broadcasted_iota(jnp.int32, (tm, LANES), 1)
    work = jnp.where(lane < n_experts, lg_ref[...], -jnp.inf)

    vals, hots = [], []
    for _ in range(TOP_K):
        m = jnp.max(work, axis=-1, keepdims=True)
        first = jnp.min(jnp.where(work == m, lane, LANES), axis=-1, keepdims=True)
        hot = lane == first
        vals.append(m)
        hots.append(hot)
        work = jnp.where(hot, -jnp.inf, work)
    chosen = jnp.zeros((tm, LANES), F32)
    for hot in hots:
        chosen = chosen + hot.astype(F32)

    @pl.when((phase == 0) & (i == 0))
    def _():
        carry_scr[...] = jnp.zeros(carry_scr.shape, F32)

    @pl.when((phase == 1) & (i == 0))
    def _():
        counts = carry_scr[...]
        padded = jnp.ceil(counts / tile) * tile
        base_scr[...] = _lane_excl_cumsum(padded)
        cnt_ref[...] = counts
        carry_scr[...] = jnp.zeros(carry_scr.shape, F32)

    @pl.when(phase == 1)
    def _():
        r = lax.broadcasted_iota(jnp.int32, (tm, tm), 0)
        c = lax.broadcasted_iota(jnp.int32, (tm, tm), 1)
        tri = (c < r).astype(BF16)
        before = jnp.dot(tri, chosen.astype(BF16), preferred_element_type=F32)
        slot = before + carry_scr[...] + base_scr[...]
        exps = [jnp.exp(v - vals[0]) for v in vals]
        den = exps[0] + exps[1] + exps[2] + exps[3]
        for k in range(TOP_K):
            dk = jnp.sum(jnp.where(hots[k], slot, 0.0), axis=-1, keepdims=True)
            dest_ref[:, k:k + 1] = dk.astype(jnp.int32)
            gate_ref[:, k:k + 1] = exps[k] / den

    carry_scr[...] = carry_scr[...] + jnp.sum(chosen, axis=0, keepdims=True)


def _route(logits, d: Dims):
    t = logits.shape[0]
    tm = d.tm_route
    kern = functools.partial(_route_kernel, tm=tm, n_experts=d.n_experts, tile=d.tm_moe)
    return pl.pallas_call(
        kern,
        out_shape=(jax.ShapeDtypeStruct((t, TOP_K), jnp.int32),
                   jax.ShapeDtypeStruct((t, TOP_K), F32),
                   jax.ShapeDtypeStruct((1, LANES), F32)),
        grid=(2, t // tm),
        in_specs=[pl.BlockSpec((tm, LANES), lambda p, i: (i, 0))],
        out_specs=(pl.BlockSpec((tm, TOP_K), lambda p, i: (i * p, 0)),
                   pl.BlockSpec((tm, TOP_K), lambda p, i: (i * p, 0)),
                   pl.BlockSpec((1, LANES), lambda p, i: (0, 0))),
        scratch_shapes=[pltpu.VMEM((1, LANES), F32), pltpu.VMEM((1, LANES), F32)],
        compiler_params=_cparams(("arbitrary", "arbitrary")),
        name="route",
    )(logits)


def _row_copy(src_ref, src_row, dst_ref, dst_row, sem):
    shift, low = SUBLANES.bit_length() - 1, SUBLANES - 1
    return pltpu.make_async_copy(
        src_ref.at[src_row >> shift, :, pl.ds(src_row & low, 1), :],
        dst_ref.at[dst_row >> shift, :, pl.ds(dst_row & low, 1), :], sem)


def _dispatch_kernel(dest_ref, h4_ref, xs_ref, sem, *, tm):
    i = pl.program_id(0)
    n = pl.num_programs(0)

    def copies(blk, r):
        tok = blk * tm + r
        return [_row_copy(h4_ref, tok, xs_ref, dest_ref[tok * TOP_K + k], sem) for k in range(TOP_K)]

    def issue(r, carry):
        for cp in copies(i, r):
            cp.start()
        return carry

    lax.fori_loop(0, tm, issue, 0)

    def drain(blk):
        def body(r, carry):
            for cp in copies(blk, r):
                cp.wait()
            return carry
        lax.fori_loop(0, tm, body, 0)

    @pl.when(i > 0)
    def _():
        drain(i - 1)

    @pl.when(i == n - 1)
    def _():
        drain(i)


def _dispatch(dest_flat, h4, n_rows, d: Dims):
    t = h4.shape[0] * SUBLANES
    tm = d.tm_disp
    kern = functools.partial(_dispatch_kernel, tm=tm)
    return pl.pallas_call(
        kern,
        out_shape=jax.ShapeDtypeStruct((n_rows // SUBLANES,) + h4.shape[1:], F32),
        grid_spec=pltpu.PrefetchScalarGridSpec(
            num_scalar_prefetch=1,
            grid=(t // tm,),
            in_specs=[pl.BlockSpec(memory_space=pl.ANY)],
            out_specs=pl.BlockSpec(memory_space=pl.ANY),
            scratch_shapes=[pltpu.SemaphoreType.DMA],
        ),
        compiler_params=_cparams(("arbitrary",)),
        name="dispatch",
    )(dest_flat, h4)


def _moe_kernel(te_ref, ts_ref, tv_ref, nu_ref,
                xs_ref, wg_ref, wu_ref, bg_ref, bu_ref, wd_ref, bd_ref,
                y_ref, xb_scr, *, tm):
    i = pl.program_id(0)
    j = pl.program_id(1)
    nc = xs_ref.shape[1]

    @pl.when(i < nu_ref[0])
    def _():
        @pl.when(j == 0)
        def _():
            valid = _row_tiles(lax.broadcasted_iota(jnp.int32, (tm, LANES), 0)) < tv_ref[i]
            for c in range(nc):
                xb_scr[:, c * LANES:(c + 1) * LANES] = (
                    jnp.where(valid, xs_ref[:, c], 0.0).reshape(tm, LANES).astype(BF16))
                y_ref[:, c] = jnp.broadcast_to(bd_ref[0, :, c * LANES:(c + 1) * LANES],
                                               (tm // SUBLANES, SUBLANES, LANES))

        x = xb_scr[...]
        gate = jnp.dot(x, wg_ref[0], preferred_element_type=F32) + bg_ref[0]
        up = jnp.dot(x, wu_ref[0], preferred_element_type=F32) + bu_ref[0]
        gate = jnp.minimum(gate, SWIGLU_LIMIT)
        up = jnp.clip(up, -SWIGLU_LIMIT, SWIGLU_LIMIT)
        act = ((up + 1.0) * gate * _sigmoid(SWIGLU_ALPHA * gate)).astype(BF16)
        for c in range(0, nc, 2):
            part = jnp.dot(act, wd_ref[0, :, c * LANES:(c + 2) * LANES], preferred_element_type=F32)
            y_ref[:, c] = y_ref[:, c] + _row_tiles(part[:, :LANES])
            y_ref[:, c + 1] = y_ref[:, c + 1] + _row_tiles(part[:, LANES:])


def _moe(tile_expert, tile_src, tile_valid, n_used, xs4, wgu, bgu, wd, bd, d: Dims):
    n_rows = xs4.shape[0] * SUBLANES
    tm, tf = d.tm_moe, d.tf_moe
    dm, ff = d.d_model, d.d_ff
    nj = ff // tf
    n_tiles = n_rows // tm
    blk4 = (tm // SUBLANES,) + xs4.shape[1:]

    def jj(i, j, nu):
        return jnp.where(i < nu[0], j, nj - 1)

    kern = functools.partial(_moe_kernel, tm=tm)
    return pl.pallas_call(
        kern,
        out_shape=jax.ShapeDtypeStruct(xs4.shape, F32),
        grid_spec=pltpu.PrefetchScalarGridSpec(
            num_scalar_prefetch=4,
            grid=(n_tiles, nj),
            in_specs=[
                pl.BlockSpec(blk4, lambda i, j, te, ts, tv, nu: (ts[i], 0, 0, 0)),
                pl.BlockSpec((1, dm, tf), lambda i, j, te, ts, tv, nu: (te[i], 0, jj(i, j, nu))),
                pl.BlockSpec((1, dm, tf), lambda i, j, te, ts, tv, nu: (te[i], 0, nj + jj(i, j, nu))),
                pl.BlockSpec((1, 1, tf), lambda i, j, te, ts, tv, nu: (te[i], 0, jj(i, j, nu))),
                pl.BlockSpec((1, 1, tf), lambda i, j, te, ts, tv, nu: (te[i], 0, nj + jj(i, j, nu))),
                pl.BlockSpec((1, tf, dm), lambda i, j, te, ts, tv, nu: (te[i], jj(i, j, nu), 0)),
                pl.BlockSpec((1, 1, dm), lambda i, j, te, ts, tv, nu: (te[i], 0, 0)),
            ],
            out_specs=pl.BlockSpec(blk4, lambda i, j, te, ts, tv, nu: (ts[i], 0, 0, 0)),
            scratch_shapes=[pltpu.VMEM((tm, dm), BF16)],
        ),
        compiler_params=_cparams(("arbitrary", "arbitrary")),
        name="moe",
    )(tile_expert, tile_src, tile_valid, n_used, xs4, wgu, wgu, bgu, bgu, wd, bd)


def _combine_kernel(dest_ref, y4_ref, gate_ref, x1_ref, o_ref, buf_a, buf_b, sem_a, sem_b, *, tm):
    i = pl.program_id(0)
    n = pl.num_programs(0)

    def copies(half_blk, r, buf, sem):
        tok = half_blk * tm + r
        return [_row_copy(y4_ref, dest_ref[tok * TOP_K + k], buf, k * tm + r, sem) for k in range(TOP_K)]

    def issue(half_blk, buf, sem):
        def body(r, carry):
            for cp in copies(half_blk, r, buf, sem):
                cp.start()
            return carry
        lax.fori_loop(0, tm, body, 0)

    def drain(half_blk, buf, sem):
        def body(r, carry):
            for cp in copies(half_blk, r, buf, sem):
                cp.wait()
            return carry
        lax.fori_loop(0, tm, body, 0)

    def reduce(buf, row0):
        gates = gate_ref[row0:row0 + tm, :]
        tg = tm // SUBLANES
        for c in range(buf.shape[1]):
            acc = x1_ref[row0:row0 + tm, c * LANES:(c + 1) * LANES]
            for k in range(TOP_K):
                acc = acc + gates[:, k:k + 1] * buf[k * tg:(k + 1) * tg, c].reshape(tm, LANES)
            o_ref[row0:row0 + tm, c * LANES:(c + 1) * LANES] = acc

    @pl.when(i == 0)
    def _():
        issue(0, buf_a, sem_a)

    issue(2 * i + 1, buf_b, sem_b)
    drain(2 * i, buf_a, sem_a)
    reduce(buf_a, 0)

    @pl.when(i < n - 1)
    def _():
        issue(2 * i + 2, buf_a, sem_a)

    drain(2 * i + 1, buf_b, sem_b)
    reduce(buf_b, tm)


def _combine(dest_flat, y4, gates, x1, d: Dims):
    t, dm = x1.shape
    tm = d.tm_comb
    kern = functools.partial(_combine_kernel, tm=tm)
    buf = pltpu.VMEM((TOP_K * tm // SUBLANES,) + y4.shape[1:], F32)
    return pl.pallas_call(
        kern,
        out_shape=jax.ShapeDtypeStruct((t, dm), F32),
        grid_spec=pltpu.PrefetchScalarGridSpec(
            num_scalar_prefetch=1,
            grid=(t // (2 * tm),),
            in_specs=[
                pl.BlockSpec(memory_space=pl.ANY),
                pl.BlockSpec((2 * tm, TOP_K), lambda i, dest: (i, 0)),
                pl.BlockSpec((2 * tm, dm), lambda i, dest: (i, 0)),
            ],
            out_specs=pl.BlockSpec((2 * tm, dm), lambda i, dest: (i, 0)),
            scratch_shapes=[buf, buf, pltpu.SemaphoreType.DMA, pltpu.SemaphoreType.DMA],
        ),
        compiler_params=_cparams(("arbitrary",)),
        name="combine",
    )(dest_flat, y4, gates, x1)


def _pad_lanes(v, fill=0.0):
    v = v.reshape(1, -1).astype(F32)
    return jnp.pad(v, ((0, 0), (0, LANES - v.shape[1])), constant_values=fill)


def _layer(d: Dims, x, positions, g_mix, w_in, g_q_latent, w_uq, g_kv_latent, w_ukv,
           g_q_nope, g_q_rope, g_k_nope, g_k_rope, w_pool, s_pool, w_out,
           g_ffn, w_router, b_router, w_gate_up, b_gate_up, w_down, b_down):
    b, s, dm, h = d.batch, d.seq, d.d_model, d.n_heads
    t = b * s
    assert d.nope == LANES and d.v_dim == LANES and d.rope <= LANES and dm % LANES == 0
    assert d.q_rank == d.kv_rank and d.pool_width % d.q_rank == 0 and dm % d.pool_width == 0
    assert d.tq % d.tk == 0 and s % d.tq == 0

    o_q, o_kv = 0, d.q_rank
    o_kr = o_kv + d.kv_rank
    o_pool = o_kr + d.rope
    o_ga = o_pool + d.pool_width
    o_gp = o_ga + dm
    n_pool = -(-(d.q_rank + d.kv_rank) // d.pool_width) * d.pool_width
    n_ga = -(-(n_pool + d.pool_width) // dm) * dm
    n_gp = n_ga + dm
    n_kr = n_gp + dm
    nz = -(-(n_kr + LANES) // d.tn_in) * d.tn_in
    zeros = lambda n: jnp.zeros((dm, n), w_in.dtype)
    w_in_r = jnp.concatenate([
        w_in[:, o_q:o_kr], zeros(n_pool - (d.q_rank + d.kv_rank)),
        w_in[:, o_pool:o_ga], zeros(n_ga - n_pool - d.pool_width),
        w_in[:, o_ga:o_gp], w_in[:, o_gp:],
        w_in[:, o_kr:o_pool], zeros(nz - n_kr - d.rope)], axis=1).astype(BF16)

    qk = d.nope + d.rope
    half = d.rope // 2
    wq3 = w_uq.reshape(d.q_rank, h, qk)
    wq = jnp.concatenate([wq3, jnp.zeros((d.q_rank, h, 2 * LANES - qk), w_uq.dtype)], axis=2)
    wqt = wq.reshape(d.q_rank, h * 2 * LANES).T.astype(BF16)
    wkv3 = w_ukv.reshape(d.kv_rank, h, d.nope + d.v_dim)
    wk = wkv3[:, :, :d.nope].reshape(d.kv_rank, h * d.nope).astype(BF16)
    wvt = wkv3[:, :, d.nope:].reshape(d.kv_rank, h * d.v_dim).T.astype(BF16)

    inv_freq = ROPE_THETA ** (-np.arange(0, d.rope, 2, dtype=np.float32) / d.rope)
    invf = np.zeros((1, LANES), np.float32)
    invf[0, :half] = inv_freq
    invf[0, half:d.rope] = inv_freq
    sgn = np.zeros((1, LANES), np.float32)
    sgn[0, :half] = -1.0
    sgn[0, half:d.rope] = 1.0
    tmq = d.tk
    invft = np.ascontiguousarray(np.broadcast_to(inv_freq[:, None], (half, tmq)))
    consts = (jnp.asarray(invf), jnp.asarray(sgn), jnp.asarray(invft))
    bcast = lambda v: jnp.broadcast_to(v.astype(F32)[:, None], (v.shape[0], tmq))

    x2d = x.reshape(t, dm)
    pos_f = positions.reshape(t).astype(F32)

    z = _inproj(x2d, g_mix.reshape(1, dm), w_in_r, d)
    qt, k, vt = _qkv(z, pos_f.reshape(t, 1), pos_f.reshape(1, t), consts,
                     g_q_latent.reshape(1, -1), g_kv_latent.reshape(1, -1),
                     bcast(g_q_nope), bcast(g_q_rope[:half]), bcast(g_q_rope[half:]),
                     _pad_lanes(g_k_nope), _pad_lanes(g_k_rope), wqt, wk, wvt, d, n_kr)
    ya = _attn(qt, k, vt, d)
    wr = jnp.pad(w_router.astype(F32), ((0, 0), (0, LANES - d.n_experts)))
    wr_hi = wr.astype(BF16)
    wr = jnp.concatenate([wr_hi, (wr - wr_hi.astype(F32)).astype(BF16)], axis=1)
    x1, h4, logits = _mixout(z, ya, x2d, w_pool.astype(BF16), s_pool.reshape(1, dm), w_out.astype(BF16),
                             g_ffn.reshape(1, dm), wr, _pad_lanes(b_router), d, n_pool, n_ga, n_gp)

    dest, gates, counts = _route(logits, d)
    tm = d.tm_moe
    n_rows = t * TOP_K + d.n_experts * tm
    n_tiles = n_rows // tm
    cnt = counts[0, :d.n_experts].astype(jnp.int32)
    ends = jnp.cumsum((cnt + tm - 1) // tm * tm)
    n_used = ends[-1] // tm
    tile_ids = jnp.minimum(jnp.arange(n_tiles, dtype=jnp.int32), n_used - 1)
    tile_expert = jnp.sum(ends[None, :] <= (tile_ids * tm)[:, None], axis=1).astype(jnp.int32)
    tile_expert = jnp.minimum(tile_expert, d.n_experts - 1)
    starts = ends - (cnt + tm - 1) // tm * tm
    tile_valid = jnp.clip(starts[tile_expert] + cnt[tile_expert] - tile_ids * tm, 0, tm).astype(jnp.int32)

    dest_flat = dest.reshape(t * TOP_K)
    xs4 = _dispatch(dest_flat, h4, n_rows, d)
    y4 = _moe(tile_expert, tile_ids, tile_valid, n_used.reshape(1).astype(jnp.int32), xs4,
              w_gate_up.astype(BF16), b_gate_up.reshape(d.n_experts, 1, 2 * d.d_ff),
              w_down.astype(BF16), b_down.reshape(d.n_experts, 1, dm), d)
    out = _combine(dest_flat, y4, gates, x1, d)
    return out.reshape(b, s, dm)


def kernel(x, positions, g_mix, w_in, g_q_latent, w_uq, g_kv_latent, w_ukv, g_q_nope, g_q_rope, g_k_nope, g_k_rope, w_pool, s_pool, w_out, g_ffn, w_router, b_router, w_gate_up, b_gate_up, w_down, b_down):
    return _layer(Dims(), x, positions, g_mix, w_in, g_q_latent, w_uq, g_kv_latent, w_ukv,
                  g_q_nope, g_q_rope, g_k_nope, g_k_rope, w_pool, s_pool, w_out,
                  g_ffn, w_router, b_router, w_gate_up, b_gate_up, w_down, b_down)
```

```python
import functools
import math
from typing import NamedTuple

import numpy as np
import jax
import jax.numpy as jnp
from jax import lax
from jax.experimental import pallas as pl
from jax.experimental.pallas import tpu as pltpu

F32 = jnp.float32
BF16 = jnp.bfloat16

RMS_EPS = 1e-6
NEG_INF = -1e30
ROPE_THETA = 10000.0
SWIGLU_LIMIT = 7.0
SWIGLU_ALPHA = 1.702
POOL_WINDOWS = (2, 4, 8, 16)
TOP_K = 4

LANES = 128
SUBLANES = 8
POOL_HALO = 16
VMEM_LIMIT = 56 * 1024 * 1024


class Dims(NamedTuple):
    batch: int = 2
    seq: int = 8192
    d_model: int = 2048
    n_heads: int = 16
    nope: int = 128
    rope: int = 64
    v_dim: int = 128
    q_rank: int = 512
    kv_rank: int = 512
    pool_width: int = 1024
    n_experts: int = 32
    d_ff: int = 2048
    tm_in: int = 1024
    tn_in: int = 1664
    tq: int = 2048
    tk: int = 512
    attn_gw: int = 512
    tm_mix: int = 256
    tm_route: int = 512
    tm_disp: int = 256
    tm_moe: int = 512
    tf_moe: int = 512
    tm_comb: int = 128


def _cparams(sem, vmem=VMEM_LIMIT):
    return pltpu.CompilerParams(dimension_semantics=sem, vmem_limit_bytes=vmem)


def _const_spec(shape):
    nd = len(shape)
    return pl.BlockSpec(shape, lambda *_: (0,) * nd, pipeline_mode=pl.Buffered(1))


def _rms(t, gain):
    return t * lax.rsqrt(jnp.mean(t * t, axis=-1, keepdims=True) + RMS_EPS) * gain


def _sigmoid(t):
    return 0.5 * jnp.tanh(0.5 * t) + 0.5


def _row_tiles(v):
    return v.reshape(v.shape[0] // SUBLANES, SUBLANES, LANES)


def _inproj_kernel(x_ref, g_ref, w_ref, z_ref, h_scr):
    @pl.when(pl.program_id(1) == 0)
    def _():
        h_scr[...] = _rms(x_ref[...], g_ref[...]).astype(BF16)

    z_ref[...] = jnp.dot(h_scr[...], w_ref[...], preferred_element_type=F32).astype(z_ref.dtype)


def _inproj(x2d, g_mix, w_in_r, d: Dims):
    t, dm = x2d.shape
    nz = w_in_r.shape[1]
    tm, tn = d.tm_in, d.tn_in
    return pl.pallas_call(
        _inproj_kernel,
        out_shape=jax.ShapeDtypeStruct((t, nz), BF16),
        grid=(t // tm, nz // tn),
        in_specs=[
            pl.BlockSpec((tm, dm), lambda i, j: (i, 0)),
            pl.BlockSpec((1, dm), lambda i, j: (0, 0)),
            pl.BlockSpec((dm, tn), lambda i, j: (0, j)),
        ],
        out_specs=pl.BlockSpec((tm, tn), lambda i, j: (i, j)),
        scratch_shapes=[pltpu.VMEM((tm, dm), BF16)],
        compiler_params=_cparams(("parallel", "arbitrary")),
        name="inproj",
    )(x2d, g_mix, w_in_r)


def _rope_tile(t, cosf, sinf, half):
    lane = lax.broadcasted_iota(jnp.int32, t.shape, 1)
    swapped = jnp.where(lane < half, pltpu.roll(t, LANES - half, 1), pltpu.roll(t, half, 1))
    return t * cosf + swapped * sinf


def _qkv_kernel(zq_ref, zkv_ref, zkr_ref, pos_ref, posr_ref, invf_ref, sgn_ref, invft_ref,
                gql_ref, gkvl_ref, gqnt_ref, gqr1_ref, gqr2_ref, gkn_ref, gkr_ref,
                wqt_ref, wk_ref, wvt_ref,
                qt_ref, k_ref, vt_ref, *, n_heads, rope, qscale):
    half = rope // 2
    hw = 2 * LANES
    ang = pos_ref[...] * invf_ref[...]
    cosf = jnp.cos(ang)
    sinf = jnp.sin(ang) * sgn_ref[...]
    angt = invft_ref[...] * posr_ref[...]
    cost = jnp.cos(angt)
    sint = jnp.sin(angt)

    hq = _rms(zq_ref[...].astype(F32), gql_ref[...])
    hkv = _rms(zkv_ref[...].astype(F32), gkvl_ref[...])
    hqt = hq.T.astype(BF16)
    hkvt = hkv.T.astype(BF16)
    hkv = hkv.astype(BF16)

    kr = zkr_ref[...].astype(F32)
    kr = kr * lax.rsqrt(jnp.sum(kr * kr, axis=-1, keepdims=True) / rope + RMS_EPS) * gkr_ref[...]
    kpe = _rope_tile(kr, cosf, sinf, half).astype(BF16)

    for h in range(n_heads):
        qh = jnp.dot(wqt_ref[h * hw:(h + 1) * hw, :], hqt, preferred_element_type=F32)
        qn = qh[:LANES]
        qn = qn * lax.rsqrt(jnp.mean(qn * qn, axis=0, keepdims=True) + RMS_EPS) * gqnt_ref[...]
        t1 = qh[LANES:LANES + half]
        t2 = qh[LANES + half:LANES + rope]
        ss = jnp.sum(t1 * t1, axis=0, keepdims=True) + jnp.sum(t2 * t2, axis=0, keepdims=True)
        r = lax.rsqrt(ss / rope + RMS_EPS)
        t1 = t1 * r * gqr1_ref[...]
        t2 = t2 * r * gqr2_ref[...]
        qt_ref[h * hw:h * hw + LANES, :] = (qn * qscale).astype(BF16)
        qt_ref[h * hw + LANES:h * hw + LANES + half, :] = ((t1 * cost - t2 * sint) * qscale).astype(BF16)
        qt_ref[h * hw + LANES + half:h * hw + LANES + rope, :] = ((t2 * cost + t1 * sint) * qscale).astype(BF16)
        qt_ref[h * hw + LANES + rope:(h + 1) * hw, :] = jnp.zeros((LANES - rope, qt_ref.shape[1]), BF16)

        kh = jnp.dot(hkv, wk_ref[:, h * LANES:(h + 1) * LANES], preferred_element_type=F32)
        k_ref[:, h * hw:h * hw + LANES] = _rms(kh, gkn_ref[...]).astype(BF16)
        k_ref[:, h * hw + LANES:(h + 1) * hw] = kpe

    for c in range(wvt_ref.shape[0] // hw):
        vt = jnp.dot(wvt_ref[c * hw:(c + 1) * hw, :], hkvt, preferred_element_type=F32)
        vt_ref[0, c * hw:(c + 1) * hw, :] = vt.astype(BF16)


def _qkv(z, pos_col, pos_row, consts, gql, gkvl, gqnt, gqr1, gqr2, gkn, gkr, wqt, wk, wvt, d: Dims, off_kr):
    t = z.shape[0]
    tm = d.tk
    invf, sgn, invft = consts
    hq = d.n_heads * 2 * LANES
    hv = d.n_heads * d.v_dim
    row = lambda a: pl.BlockSpec(a.shape, lambda i: (0, 0))
    kern = functools.partial(_qkv_kernel, n_heads=d.n_heads, rope=d.rope,
                             qscale=math.log2(math.e) / math.sqrt(d.nope + d.rope))
    return pl.pallas_call(
        kern,
        out_shape=(jax.ShapeDtypeStruct((hq, t), BF16),
                   jax.ShapeDtypeStruct((t, hq), BF16),
                   jax.ShapeDtypeStruct((t // tm, hv, tm), BF16)),
        grid=(t // tm,),
        in_specs=[
            pl.BlockSpec((tm, d.q_rank), lambda i: (i, 0)),
            pl.BlockSpec((tm, d.kv_rank), lambda i: (i, d.q_rank // d.kv_rank)),
            pl.BlockSpec((tm, LANES), lambda i: (i, off_kr // LANES)),
            pl.BlockSpec((tm, 1), lambda i: (i, 0)),
            pl.BlockSpec((1, tm), lambda i: (0, i)),
            row(invf), row(sgn), row(invft),
            row(gql), row(gkvl), row(gqnt), row(gqr1), row(gqr2), row(gkn), row(gkr),
            _const_spec(wqt.shape), _const_spec(wk.shape), _const_spec(wvt.shape),
        ],
        out_specs=(pl.BlockSpec((hq, tm), lambda i: (0, i)),
                   pl.BlockSpec((tm, hq), lambda i: (i, 0)),
                   pl.BlockSpec((1, hv, tm), lambda i: (i, 0, 0))),
        compiler_params=_cparams(("parallel",)),
        name="qkv",
    )(z, z, z, pos_col, pos_row, invf, sgn, invft, gql, gkvl, gqnt, gqr1, gqr2, gkn, gkr, wqt, wk, wvt)


def _attn_kernel(qt_ref, k_ref, vt_ref, o_ref, *scr, tq, tk, gw):
    i = pl.program_id(2)
    n_sub = tq // tk
    ng = tq // gw
    m_scr, l_scr, acc_scr = scr[:ng], scr[ng:2 * ng], scr[2 * ng:3 * ng]
    s_scr, cm_scr = scr[3 * ng:4 * ng], scr[4 * ng:]
    for g in range(ng):
        m_scr[g][...] = jnp.full(m_scr[g].shape, NEG_INF, F32)
        l_scr[g][...] = jnp.zeros(l_scr[g].shape, F32)
        acc_scr[g][...] = jnp.zeros(acc_scr[g].shape, F32)

    def scores(c, g):
        kc = k_ref[pl.ds(pl.multiple_of(c * tk, tk), tk), :]
        return jnp.dot(kc, qt_ref[:, g * gw:(g + 1) * gw], preferred_element_type=F32)

    def softmax_pv(c, g, s, cmax):
        m_old = m_scr[g][...]
        m_new = jnp.maximum(m_old, cmax)
        p = jnp.exp2(s - m_new)
        alpha = jnp.exp2(m_old - m_new)
        l_scr[g][...] = alpha * l_scr[g][...] + jnp.sum(p, axis=0, keepdims=True)
        pv = jnp.dot(vt_ref[c], p.astype(BF16), preferred_element_type=F32)
        acc_scr[g][...] = alpha * acc_scr[g][...] + pv
        m_scr[g][...] = m_new

    def stash(g, s):
        s_scr[g][...] = s
        cm_scr[g][...] = jnp.max(s, axis=0, keepdims=True)

    n_full = i * n_sub

    @pl.when(n_full > 0)
    def _():
        for g in range(ng):
            stash(g, scores(0, g))

        def body(c, carry):
            nxt = [scores(c + 1, g) for g in range(ng)]
            for g in range(ng):
                softmax_pv(c, g, s_scr[g][...], cm_scr[g][...])
            for g in range(ng):
                stash(g, nxt[g])
            return carry

        lax.fori_loop(0, n_full - 1, body, 0)
        for g in range(ng):
            softmax_pv(n_full - 1, g, s_scr[g][...], cm_scr[g][...])

    key = lax.broadcasted_iota(jnp.int32, (tk, gw), 0)
    qry = lax.broadcasted_iota(jnp.int32, (tk, gw), 1)
    steps = []
    for cc in range(n_sub):
        for g in range(ng):
            k_lo, k_hi = cc * tk, (cc + 1) * tk - 1
            q_lo, q_hi = g * gw, (g + 1) * gw - 1
            if k_lo > q_hi:
                continue
            steps.append((cc, g, k_hi > q_lo, k_lo, q_lo))

    def diag_scores(step):
        cc, g, masked, k_lo, q_lo = step
        s = scores(n_full + cc, g)
        if masked:
            s = jnp.where((key + k_lo) <= (qry + q_lo), s, NEG_INF)
        return s

    s_cur = diag_scores(steps[0])
    for j, step in enumerate(steps):
        s_nxt = diag_scores(steps[j + 1]) if j + 1 < len(steps) else None
        softmax_pv(n_full + step[0], step[1], s_cur, jnp.max(s_cur, axis=0, keepdims=True))
        s_cur = s_nxt

    for g in range(ng):
        o = acc_scr[g][...] / l_scr[g][...]
        o_ref[g * gw:(g + 1) * gw, :] = o.T.astype(o_ref.dtype)


def _attn(qt, k, vt, d: Dims):
    b, s, h = d.batch, d.seq, d.n_heads
    tq, tk = d.tq, d.tk
    gw = min(d.attn_gw, tq)
    ng = tq // gw
    nq = s // tq
    kern = functools.partial(_attn_kernel, tq=tq, tk=tk, gw=gw)
    return pl.pallas_call(
        kern,
        out_shape=jax.ShapeDtypeStruct((b * s, h * d.v_dim), BF16),
        grid=(b, h, nq),
        in_specs=[
            pl.BlockSpec((2 * LANES, tq), lambda bi, hi, i: (hi, bi * nq + i)),
            pl.BlockSpec((s, 2 * LANES), lambda bi, hi, i: (bi, hi)),
            pl.BlockSpec((s // tk, d.v_dim, tk), lambda bi, hi, i: (bi, hi, 0)),
        ],
        out_specs=pl.BlockSpec((tq, d.v_dim), lambda bi, hi, i: (bi * nq + i, hi)),
        scratch_shapes=([pltpu.VMEM((1, gw), F32)] * (2 * ng) + [pltpu.VMEM((d.v_dim, gw), F32)] * ng
                        + [pltpu.VMEM((tk, gw), F32)] * ng + [pltpu.VMEM((1, gw), F32)] * ng),
        compiler_params=_cparams(("parallel", "parallel", "arbitrary")),
        name="attn",
    )(qt, k, vt)


def _mixout_kernel(u_ref, halo_ref, ga_ref, gp_ref, ya_ref, x_ref,
                   wpool_ref, spool_ref, wout_ref, gffn_ref, wr_ref, br_ref,
                   x1_ref, h4_ref, lg_ref, merged_scr, *, tm, seq, n_groups):
    i = pl.program_id(0)
    row0 = (i * tm) % seq
    gdim = u_ref.shape[1] // n_groups
    odim = ga_ref.shape[1] // n_groups
    pos = row0 + lax.broadcasted_iota(jnp.int32, (tm, 1), 0)
    halo_on = (row0 > 0).astype(F32)

    for g in range(n_groups):
        w = POOL_WINDOWS[g]
        ug = u_ref[:, g * gdim:(g + 1) * gdim].astype(F32)
        hg = halo_ref[:, g * gdim:(g + 1) * gdim].astype(F32) * halo_on
        ext = jnp.concatenate([hg, ug], axis=0)
        shift = 1
        while shift < w:
            ext = ext + pltpu.roll(ext, shift, 0)
            shift *= 2
        wsum = ext[POOL_HALO:, :]
        count = jnp.minimum(pos + 1, w).astype(F32)
        pooled = (wsum / count - ug).astype(BF16)
        yp = jnp.dot(pooled, wpool_ref[g], preferred_element_type=F32)
        yp = yp * spool_ref[:, g * odim:(g + 1) * odim]
        sl = slice(g * odim, (g + 1) * odim)
        merged = (_sigmoid(ga_ref[:, sl].astype(F32)) * ya_ref[:, sl].astype(F32)
                  + _sigmoid(gp_ref[:, sl].astype(F32)) * yp)
        merged_scr[:, sl] = merged.astype(BF16)

    x1 = x_ref[...] + jnp.dot(merged_scr[...], wout_ref[...], preferred_element_type=F32)
    x1_ref[...] = x1
    hn = _rms(x1, gffn_ref[...])
    hi = hn.astype(BF16)
    lo = (hn - hi.astype(F32)).astype(BF16)
    r = jnp.dot(jnp.concatenate([hi, lo], axis=0), wr_ref[...], preferred_element_type=F32)
    lg_ref[...] = (r[:tm, :LANES] + r[:tm, LANES:]) + (r[tm:, :LANES] + r[tm:, LANES:]) + br_ref[...]
    for c in range(h4_ref.shape[1]):
        h4_ref[:, c] = _row_tiles(hn[:, c * LANES:(c + 1) * LANES])


def _mixout(z, ya, x2d, wpool, spool, wout, gffn, wr, br, d: Dims, off_pool, off_ga, off_gp):
    t, dm = x2d.shape
    tm = d.tm_mix
    pw = d.pool_width
    n_groups = len(POOL_WINDOWS)
    hb = tm // POOL_HALO
    nc = dm // LANES
    kern = functools.partial(_mixout_kernel, tm=tm, seq=d.seq, n_groups=n_groups)
    return pl.pallas_call(
        kern,
        out_shape=(jax.ShapeDtypeStruct((t, dm), F32),
                   jax.ShapeDtypeStruct((t // SUBLANES, nc, SUBLANES, LANES), F32),
                   jax.ShapeDtypeStruct((t, LANES), F32)),
        grid=(t // tm,),
        in_specs=[
            pl.BlockSpec((tm, pw), lambda i: (i, off_pool // pw)),
            pl.BlockSpec((POOL_HALO, pw), lambda i: (jnp.maximum(i * hb - 1, 0), off_pool // pw)),
            pl.BlockSpec((tm, dm), lambda i: (i, off_ga // dm)),
            pl.BlockSpec((tm, dm), lambda i: (i, off_gp // dm)),
            pl.BlockSpec((tm, dm), lambda i: (i, 0)),
            pl.BlockSpec((tm, dm), lambda i: (i, 0)),
            _const_spec(wpool.shape), _const_spec(spool.shape), _const_spec(wout.shape),
            _const_spec(gffn.shape), _const_spec(wr.shape), _const_spec(br.shape),
        ],
        out_specs=(pl.BlockSpec((tm, dm), lambda i: (i, 0)),
                   pl.BlockSpec((tm // SUBLANES, nc, SUBLANES, LANES), lambda i: (i, 0, 0, 0)),
                   pl.BlockSpec((tm, LANES), lambda i: (i, 0))),
        scratch_shapes=[pltpu.VMEM((tm, dm), BF16)],
        compiler_params=_cparams(("parallel",)),
        name="mixout",
    )(z, z, z, z, ya, x2d, wpool, spool, wout, gffn, wr, br)


def _lane_excl_cumsum(v):
    lane = lax.broadcasted_iota(jnp.int32, v.shape, 1)
    inc = v
    shift = 1
    while shift < LANES:
        inc = inc + jnp.where(lane >= shift, pltpu.roll(inc, shift, 1), 0.0)
        shift *= 2
    return inc - v


def _route_kernel(lg_ref, dest_ref, gate_ref, cnt_ref, carry_scr, base_scr, *, tm, n_experts, tile):
    phase = pl.program_id(0)
    i = pl.program_id(1)
    lane = lax.broadcasted_iota(jnp.int32, (tm, LANES), 1)
    work = jnp.where(lane < n_experts, lg_ref[...], -jnp.inf)

    vals, hots = [], []
    for _ in range(TOP_K):
        m = jnp.max(work, axis=-1, keepdims=True)
        first = jnp.min(jnp.where(work == m, lane, LANES), axis=-1, keepdims=True)
        hot = lane == first
        vals.append(m)
        hots.append(hot)
        work = jnp.where(hot, -jnp.inf, work)
    chosen = jnp.zeros((tm, LANES), F32)
    for hot in hots:
        chosen = chosen + hot.astype(F32)

    @pl.when((phase == 0) & (i == 0))
    def _():
        carry_scr[...] = jnp.zeros(carry_scr.shape, F32)

    @pl.when((phase == 1) & (i == 0))
    def _():
        counts = carry_scr[...]
        padded = jnp.ceil(counts / tile) * tile
        base_scr[...] = _lane_excl_cumsum(padded)
        cnt_ref[...] = counts
        carry_scr[...] = jnp.zeros(carry_scr.shape, F32)

    @pl.when(phase == 1)
    def _():
        r = lax.broadcasted_iota(jnp.int32, (tm, tm), 0)
        c = lax.broadcasted_iota(jnp.int32, (tm, tm), 1)
        tri = (c < r).astype(BF16)
        before = jnp.dot(tri, chosen.astype(BF16), preferred_element_type=F32)
        slot = before + carry_scr[...] + base_scr[...]
        exps = [jnp.exp(v - vals[0]) for v in vals]
        den = exps[0] + exps[1] + exps[2] + exps[3]
        for k in range(TOP_K):
            dk = jnp.sum(jnp.where(hots[k], slot, 0.0), axis=-1, keepdims=True)
            dest_ref[:, k:k + 1] = dk.astype(jnp.int32)
            gate_ref[:, k:k + 1] = exps[k] / den

    carry_scr[...] = carry_scr[...] + jnp.sum(chosen, axis=0, keepdims=True)


def _route(logits, d: Dims):
    t = logits.shape[0]
    tm = d.tm_route
    kern = functools.partial(_route_kernel, tm=tm, n_experts=d.n_experts, tile=d.tm_moe)
    return pl.pallas_call(
        kern,
        out_shape=(jax.ShapeDtypeStruct((t, TOP_K), jnp.int32),
                   jax.ShapeDtypeStruct((t, TOP_K), F32),
                   jax.ShapeDtypeStruct((1, LANES), F32)),
        grid=(2, t // tm),
        in_specs=[pl.BlockSpec((tm, LANES), lambda p, i: (i, 0))],
        out_specs=(pl.BlockSpec((tm, TOP_K), lambda p, i: (i * p, 0)),
                   pl.BlockSpec((tm, TOP_K), lambda p, i: (i * p, 0)),
                   pl.BlockSpec((1, LANES), lambda p, i: (0, 0))),
        scratch_shapes=[pltpu.VMEM((1, LANES), F32), pltpu.VMEM((1, LANES), F32)],
        compiler_params=_cparams(("arbitrary", "arbitrary")),
        name="route",
    )(logits)


def _row_copy(src_ref, src_row, dst_ref, dst_row, sem):
    shift, low = SUBLANES.bit_length() - 1, SUBLANES - 1
    return pltpu.make_async_copy(
        src_ref.at[src_row >> shift, :, pl.ds(src_row & low, 1), :],
        dst_ref.at[dst_row >> shift, :, pl.ds(dst_row & low, 1), :], sem)


def _dispatch_kernel(dest_ref, h4_ref, xs_ref, sem, *, tm):
    i = pl.program_id(0)

    def copies(r):
        tok = i * tm + r
        return [_row_copy(h4_ref, r, xs_ref, dest_ref[tok * TOP_K + k], sem) for k in range(TOP_K)]

    def issue(r, carry):
        for cp in copies(r):
            cp.start()
        return carry

    def drain(r, carry):
        for cp in copies(r):
            cp.wait()
        return carry

    lax.fori_loop(0, tm, issue, 0)
    lax.fori_loop(0, tm, drain, 0)


def _dispatch(dest_flat, h4, n_rows, d: Dims):
    t = h4.shape[0] * SUBLANES
    tm = d.tm_disp
    kern = functools.partial(_dispatch_kernel, tm=tm)
    return pl.pallas_call(
        kern,
        out_shape=jax.ShapeDtypeStruct((n_rows // SUBLANES,) + h4.shape[1:], F32),
        grid_spec=pltpu.PrefetchScalarGridSpec(
            num_scalar_prefetch=1,
            grid=(t // tm,),
            in_specs=[pl.BlockSpec((tm // SUBLANES,) + h4.shape[1:], lambda i, dest: (i, 0, 0, 0))],
            out_specs=pl.BlockSpec(memory_space=pl.ANY),
            scratch_shapes=[pltpu.SemaphoreType.DMA],
        ),
        compiler_params=_cparams(("arbitrary",)),
        name="dispatch",
    )(dest_flat, h4)


def _moe_kernel(te_ref, ts_ref, tv_ref, nu_ref,
                xs_ref, wg_ref, wu_ref, bg_ref, bu_ref, wd_ref, bd_ref,
                y_ref, xb_scr, *, tm):
    i = pl.program_id(0)
    j = pl.program_id(1)
    nc = xs_ref.shape[1]

    @pl.when(i < nu_ref[0])
    def _():
        @pl.when(j == 0)
        def _():
            valid = _row_tiles(lax.broadcasted_iota(jnp.int32, (tm, LANES), 0)) < tv_ref[i]
            for c in range(nc):
                xb_scr[:, c * LANES:(c + 1) * LANES] = (
                    jnp.where(valid, xs_ref[:, c], 0.0).reshape(tm, LANES).astype(BF16))
                y_ref[:, c] = jnp.broadcast_to(bd_ref[0, :, c * LANES:(c + 1) * LANES],
                                               (tm // SUBLANES, SUBLANES, LANES))

        x = xb_scr[...]
        gate = jnp.dot(x, wg_ref[0].astype(BF16), preferred_element_type=F32) + bg_ref[0]
        up = jnp.dot(x, wu_ref[0].astype(BF16), preferred_element_type=F32) + bu_ref[0]
        gate = jnp.minimum(gate, SWIGLU_LIMIT)
        up = jnp.clip(up, -SWIGLU_LIMIT, SWIGLU_LIMIT)
        act = ((up + 1.0) * gate * _sigmoid(SWIGLU_ALPHA * gate)).astype(BF16)
        for c in range(0, nc, 2):
            part = jnp.dot(act, wd_ref[0, :, c * LANES:(c + 2) * LANES].astype(BF16),
                           preferred_element_type=F32)
            y_ref[:, c] = y_ref[:, c] + _row_tiles(part[:, :LANES])
            y_ref[:, c + 1] = y_ref[:, c + 1] + _row_tiles(part[:, LANES:])


def _moe(tile_expert, tile_src, tile_valid, n_used, xs4, wgu, bgu, wd, bd, d: Dims):
    n_rows = xs4.shape[0] * SUBLANES
    tm, tf = d.tm_moe, d.tf_moe
    dm, ff = d.d_model, d.d_ff
    nj = ff // tf
    n_tiles = n_rows // tm
    blk4 = (tm // SUBLANES,) + xs4.shape[1:]

    def jj(i, j, nu):
        return jnp.where(i < nu[0], j, nj - 1)

    kern = functools.partial(_moe_kernel, tm=tm)
    return pl.pallas_call(
        kern,
        out_shape=jax.ShapeDtypeStruct(xs4.shape, F32),
        grid_spec=pltpu.PrefetchScalarGridSpec(
            num_scalar_prefetch=4,
            grid=(n_tiles, nj),
            in_specs=[
                pl.BlockSpec(blk4, lambda i, j, te, ts, tv, nu: (ts[i], 0, 0, 0)),
                pl.BlockSpec((1, dm, tf), lambda i, j, te, ts, tv, nu: (te[i], 0, jj(i, j, nu))),
                pl.BlockSpec((1, dm, tf), lambda i, j, te, ts, tv, nu: (te[i], 0, nj + jj(i, j, nu))),
                pl.BlockSpec((1, 1, tf), lambda i, j, te, ts, tv, nu: (te[i], 0, jj(i, j, nu))),
                pl.BlockSpec((1, 1, tf), lambda i, j, te, ts, tv, nu: (te[i], 0, nj + jj(i, j, nu))),
                pl.BlockSpec((1, tf, dm), lambda i, j, te, ts, tv, nu: (te[i], jj(i, j, nu), 0)),
                pl.BlockSpec((1, 1, dm), lambda i, j, te, ts, tv, nu: (te[i], 0, 0)),
            ],
            out_specs=pl.BlockSpec(blk4, lambda i, j, te, ts, tv, nu: (ts[i], 0, 0, 0)),
            scratch_shapes=[pltpu.VMEM((tm, dm), BF16)],
        ),
        compiler_params=_cparams(("arbitrary", "arbitrary")),
        name="moe",
    )(tile_expert, tile_src, tile_valid, n_used, xs4, wgu, wgu, bgu, bgu, wd, bd)


def _combine_kernel(dest_ref, y4_ref, gate_ref, x1_ref, o_ref, buf_a, buf_b, sem_a, sem_b, *, tm):
    i = pl.program_id(0)
    n = pl.num_programs(0)

    def copies(half_blk, r, buf, sem):
        tok = half_blk * tm + r
        return [_row_copy(y4_ref, dest_ref[tok * TOP_K + k], buf, k * tm + r, sem) for k in range(TOP_K)]

    def issue(half_blk, buf, sem):
        def body(r, carry):
            for cp in copies(half_blk, r, buf, sem):
                cp.start()
            return carry
        lax.fori_loop(0, tm, body, 0)

    def drain(half_blk, buf, sem):
        def body(r, carry):
            for cp in copies(half_blk, r, buf, sem):
                cp.wait()
            return carry
        lax.fori_loop(0, tm, body, 0)

    def reduce(buf, row0):
        gates = gate_ref[row0:row0 + tm, :]
        tg = tm // SUBLANES
        for c in range(buf.shape[1]):
            acc = x1_ref[row0:row0 + tm, c * LANES:(c + 1) * LANES]
            for k in range(TOP_K):
                acc = acc + gates[:, k:k + 1] * buf[k * tg:(k + 1) * tg, c].reshape(tm, LANES)
            o_ref[row0:row0 + tm, c * LANES:(c + 1) * LANES] = acc

    @pl.when(i == 0)
    def _():
        issue(0, buf_a, sem_a)

    issue(2 * i + 1, buf_b, sem_b)
    drain(2 * i, buf_a, sem_a)
    reduce(buf_a, 0)

    @pl.when(i < n - 1)
    def _():
        issue(2 * i + 2, buf_a, sem_a)

    drain(2 * i + 1, buf_b, sem_b)
    reduce(buf_b, tm)


def _combine(dest_flat, y4, gates, x1, d: Dims):
    t, dm = x1.shape
    tm = d.tm_comb
    kern = functools.partial(_combine_kernel, tm=tm)
    buf = pltpu.VMEM((TOP_K * tm // SUBLANES,) + y4.shape[1:], F32)
    return pl.pallas_call(
        kern,
        out_shape=jax.ShapeDtypeStruct((t, dm), F32),
        grid_spec=pltpu.PrefetchScalarGridSpec(
            num_scalar_prefetch=1,
            grid=(t // (2 * tm),),
            in_specs=[
                pl.BlockSpec(memory_space=pl.ANY),
                pl.BlockSpec((2 * tm, TOP_K), lambda i, dest: (i, 0)),
                pl.BlockSpec((2 * tm, dm), lambda i, dest: (i, 0)),
            ],
            out_specs=pl.BlockSpec((2 * tm, dm), lambda i, dest: (i, 0)),
            scratch_shapes=[buf, buf, pltpu.SemaphoreType.DMA, pltpu.SemaphoreType.DMA],
        ),
        compiler_params=_cparams(("arbitrary",)),
        name="combine",
    )(dest_flat, y4, gates, x1)


def _pad_lanes(v, fill=0.0):
    v = v.reshape(1, -1).astype(F32)
    return jnp.pad(v, ((0, 0), (0, LANES - v.shape[1])), constant_values=fill)


def _layer(d: Dims, x, positions, g_mix, w_in, g_q_latent, w_uq, g_kv_latent, w_ukv,
           g_q_nope, g_q_rope, g_k_nope, g_k_rope, w_pool, s_pool, w_out,
           g_ffn, w_router, b_router, w_gate_up, b_gate_up, w_down, b_down):
    b, s, dm, h = d.batch, d.seq, d.d_model, d.n_heads
    t = b * s
    assert d.nope == LANES and d.v_dim == LANES and d.rope <= LANES and dm % LANES == 0
    assert d.q_rank == d.kv_rank and d.pool_width % d.q_rank == 0 and dm % d.pool_width == 0
    assert d.tq % d.tk == 0 and s % d.tq == 0

    o_q, o_kv = 0, d.q_rank
    o_kr = o_kv + d.kv_rank
    o_pool = o_kr + d.rope
    o_ga = o_pool + d.pool_width
    o_gp = o_ga + dm
    n_pool = -(-(d.q_rank + d.kv_rank) // d.pool_width) * d.pool_width
    n_ga = -(-(n_pool + d.pool_width) // dm) * dm
    n_gp = n_ga + dm
    n_kr = n_gp + dm
    nz = -(-(n_kr + LANES) // d.tn_in) * d.tn_in
    zeros = lambda n: jnp.zeros((dm, n), w_in.dtype)
    w_in_r = jnp.concatenate([
        w_in[:, o_q:o_kr], zeros(n_pool - (d.q_rank + d.kv_rank)),
        w_in[:, o_pool:o_ga], zeros(n_ga - n_pool - d.pool_width),
        w_in[:, o_ga:o_gp], w_in[:, o_gp:],
        w_in[:, o_kr:o_pool], zeros(nz - n_kr - d.rope)], axis=1).astype(BF16)

    qk = d.nope + d.rope
    half = d.rope // 2
    wq3 = w_uq.reshape(d.q_rank, h, qk)
    wq = jnp.concatenate([wq3, jnp.zeros((d.q_rank, h, 2 * LANES - qk), w_uq.dtype)], axis=2)
    wqt = wq.reshape(d.q_rank, h * 2 * LANES).T.astype(BF16)
    wkv3 = w_ukv.reshape(d.kv_rank, h, d.nope + d.v_dim)
    wk = wkv3[:, :, :d.nope].reshape(d.kv_rank, h * d.nope).astype(BF16)
    wvt = wkv3[:, :, d.nope:].reshape(d.kv_rank, h * d.v_dim).T.astype(BF16)

    inv_freq = ROPE_THETA ** (-np.arange(0, d.rope, 2, dtype=np.float32) / d.rope)
    invf = np.zeros((1, LANES), np.float32)
    invf[0, :half] = inv_freq
    invf[0, half:d.rope] = inv_freq
    sgn = np.zeros((1, LANES), np.float32)
    sgn[0, :half] = -1.0
    sgn[0, half:d.rope] = 1.0
    tmq = d.tk
    invft = np.ascontiguousarray(np.broadcast_to(inv_freq[:, None], (half, tmq)))
    consts = (jnp.asarray(invf), jnp.asarray(sgn), jnp.asarray(invft))
    bcast = lambda v: jnp.broadcast_to(v.astype(F32)[:, None], (v.shape[0], tmq))

    x2d = x.reshape(t, dm)
    pos_f = positions.reshape(t).astype(F32)

    z = _inproj(x2d, g_mix.reshape(1, dm), w_in_r, d)
    qt, k, vt = _qkv(z, pos_f.reshape(t, 1), pos_f.reshape(1, t), consts,
                     g_q_latent.reshape(1, -1), g_kv_latent.reshape(1, -1),
                     bcast(g_q_nope), bcast(g_q_rope[:half]), bcast(g_q_rope[half:]),
                     _pad_lanes(g_k_nope), _pad_lanes(g_k_rope), wqt, wk, wvt, d, n_kr)
    ya = _attn(qt, k, vt, d)
    wr = jnp.pad(w_router.astype(F32), ((0, 0), (0, LANES - d.n_experts)))
    wr_hi = wr.astype(BF16)
    wr = jnp.concatenate([wr_hi, (wr - wr_hi.astype(F32)).astype(BF16)], axis=1)
    x1, h4, logits = _mixout(z, ya, x2d, w_pool.astype(BF16), s_pool.reshape(1, dm), w_out.astype(BF16),
                             g_ffn.reshape(1, dm), wr, _pad_lanes(b_router), d, n_pool, n_ga, n_gp)

    dest, gates, counts = _route(logits, d)
    tm = d.tm_moe
    n_rows = t * TOP_K + d.n_experts * tm
    n_tiles = n_rows // tm
    cnt = counts[0, :d.n_experts].astype(jnp.int32)
    ends = jnp.cumsum((cnt + tm - 1) // tm * tm)
    n_used = ends[-1] // tm
    tile_ids = jnp.minimum(jnp.arange(n_tiles, dtype=jnp.int32), n_used - 1)
    tile_expert = jnp.sum(ends[None, :] <= (tile_ids * tm)[:, None], axis=1).astype(jnp.int32)
    tile_expert = jnp.minimum(tile_expert, d.n_experts - 1)
    starts = ends - (cnt + tm - 1) // tm * tm
    tile_valid = jnp.clip(starts[tile_expert] + cnt[tile_expert] - tile_ids * tm, 0, tm).astype(jnp.int32)

    dest_flat = dest.reshape(t * TOP_K)
    xs4 = _dispatch(dest_flat, h4, n_rows, d)
    y4 = _moe(tile_expert, tile_ids, tile_valid, n_used.reshape(1).astype(jnp.int32), xs4,
              w_gate_up, b_gate_up.reshape(d.n_experts, 1, 2 * d.d_ff),
              w_down, b_down.reshape(d.n_experts, 1, dm), d)
    out = _combine(dest_flat, y4, gates, x1, d)
    return out.reshape(b, s, dm)


def kernel(x, positions, g_mix, w_in, g_q_latent, w_uq, g_kv_latent, w_ukv, g_q_nope, g_q_rope, g_k_nope, g_k_rope, w_pool, s_pool, w_out, g_ffn, w_router, b_router, w_gate_up, b_gate_up, w_down, b_down):
    return _layer(Dims(), x, positions, g_mix, w_in, g_q_latent, w_uq, g_kv_latent, w_ukv,
                  g_q_nope, g_q_rope, g_k_nope, g_k_rope, w_pool, s_pool, w_out,
                  g_ffn, w_router, b_router, w_gate_up, b_gate_up, w_down, b_down)
```

```python
import functools
import math
from typing import NamedTuple

import numpy as np
import jax
import jax.numpy as jnp
from jax import lax
from jax.experimental import pallas as pl
from jax.experimental.pallas import tpu as pltpu

F32 = jnp.float32
BF16 = jnp.bfloat16

RMS_EPS = 1e-6
NEG_INF = -1e30
ROPE_THETA = 10000.0
SWIGLU_LIMIT = 7.0
SWIGLU_ALPHA = 1.702
POOL_WINDOWS = (2, 4, 8, 16)
TOP_K = 4

LANES = 128
SUBLANES = 8
POOL_HALO = 16
VMEM_LIMIT = 56 * 1024 * 1024


class Dims(NamedTuple):
    batch: int = 2
    seq: int = 8192
    d_model: int = 2048
    n_heads: int = 16
    nope: int = 128
    rope: int = 64
    v_dim: int = 128
    q_rank: int = 512
    kv_rank: int = 512
    pool_width: int = 1024
    n_experts: int = 32
    d_ff: int = 2048
    tm_in: int = 1024
    tn_in: int = 1664
    tq: int = 2048
    tk: int = 512
    attn_gw: int = 512
    tm_mix: int = 256
    tm_route: int = 512
    tm_disp: int = 256
    tm_moe: int = 512
    tf_moe: int = 512
    tm_comb: int = 128


def _cparams(sem, vmem=VMEM_LIMIT):
    return pltpu.CompilerParams(dimension_semantics=sem, vmem_limit_bytes=vmem)


def _const_spec(shape):
    nd = len(shape)
    return pl.BlockSpec(shape, lambda *_: (0,) * nd, pipeline_mode=pl.Buffered(1))


def _rms(t, gain):
    return t * lax.rsqrt(jnp.mean(t * t, axis=-1, keepdims=True) + RMS_EPS) * gain


def _sigmoid(t):
    return 0.5 * jnp.tanh(0.5 * t) + 0.5


def _row_tiles(v):
    return v.reshape(v.shape[0] // SUBLANES, SUBLANES, LANES)


def _inproj_kernel(x_ref, g_ref, w_ref, z_ref, h_scr):
    @pl.when(pl.program_id(1) == 0)
    def _():
        h_scr[...] = _rms(x_ref[...], g_ref[...]).astype(BF16)

    z_ref[...] = jnp.dot(h_scr[...], w_ref[...], preferred_element_type=F32).astype(z_ref.dtype)


def _inproj(x2d, g_mix, w_in_r, d: Dims):
    t, dm = x2d.shape
    nz = w_in_r.shape[1]
    tm, tn = d.tm_in, d.tn_in
    return pl.pallas_call(
        _inproj_kernel,
        out_shape=jax.ShapeDtypeStruct((t, nz), BF16),
        grid=(t // tm, nz // tn),
        in_specs=[
            pl.BlockSpec((tm, dm), lambda i, j: (i, 0)),
            pl.BlockSpec((1, dm), lambda i, j: (0, 0)),
            pl.BlockSpec((dm, tn), lambda i, j: (0, j)),
        ],
        out_specs=pl.BlockSpec((tm, tn), lambda i, j: (i, j)),
        scratch_shapes=[pltpu.VMEM((tm, dm), BF16)],
        compiler_params=_cparams(("parallel", "arbitrary")),
        name="inproj",
    )(x2d, g_mix, w_in_r)


def _rope_tile(t, cosf, sinf, half):
    lane = lax.broadcasted_iota(jnp.int32, t.shape, 1)
    swapped = jnp.where(lane < half, pltpu.roll(t, LANES - half, 1), pltpu.roll(t, half, 1))
    return t * cosf + swapped * sinf


def _qkv_kernel(zq_ref, zkv_ref, zkr_ref, pos_ref, posr_ref, invf_ref, sgn_ref, invft_ref,
                gql_ref, gkvl_ref, gqnt_ref, gqr1_ref, gqr2_ref, gkn_ref, gkr_ref,
                wqt_ref, wk_ref, wvt_ref,
                qt_ref, k_ref, vt_ref, *, n_heads, rope, qscale):
    half = rope // 2
    hw = 2 * LANES
    ang = pos_ref[...] * invf_ref[...]
    cosf = jnp.cos(ang)
    sinf = jnp.sin(ang) * sgn_ref[...]
    angt = invft_ref[...] * posr_ref[...]
    cost = jnp.cos(angt)
    sint = jnp.sin(angt)

    hq = _rms(zq_ref[...].astype(F32), gql_ref[...])
    hkv = _rms(zkv_ref[...].astype(F32), gkvl_ref[...])
    hqt = hq.T.astype(BF16)
    hkvt = hkv.T.astype(BF16)
    hkv = hkv.astype(BF16)

    kr = zkr_ref[...].astype(F32)
    kr = kr * lax.rsqrt(jnp.sum(kr * kr, axis=-1, keepdims=True) / rope + RMS_EPS) * gkr_ref[...]
    kpe = _rope_tile(kr, cosf, sinf, half).astype(BF16)

    for h in range(n_heads):
        qh = jnp.dot(wqt_ref[h * hw:(h + 1) * hw, :], hqt, preferred_element_type=F32)
        qn = qh[:LANES]
        qn = qn * lax.rsqrt(jnp.mean(qn * qn, axis=0, keepdims=True) + RMS_EPS) * gqnt_ref[...]
        t1 = qh[LANES:LANES + half]
        t2 = qh[LANES + half:LANES + rope]
        ss = jnp.sum(t1 * t1, axis=0, keepdims=True) + jnp.sum(t2 * t2, axis=0, keepdims=True)
        r = lax.rsqrt(ss / rope + RMS_EPS)
        t1 = t1 * r * gqr1_ref[...]
        t2 = t2 * r * gqr2_ref[...]
        qt_ref[h * hw:h * hw + LANES, :] = (qn * qscale).astype(BF16)
        qt_ref[h * hw + LANES:h * hw + LANES + half, :] = ((t1 * cost - t2 * sint) * qscale).astype(BF16)
        qt_ref[h * hw + LANES + half:h * hw + LANES + rope, :] = ((t2 * cost + t1 * sint) * qscale).astype(BF16)
        qt_ref[h * hw + LANES + rope:(h + 1) * hw, :] = jnp.zeros((LANES - rope, qt_ref.shape[1]), BF16)

        kh = jnp.dot(hkv, wk_ref[:, h * LANES:(h + 1) * LANES], preferred_element_type=F32)
        k_ref[:, h * hw:h * hw + LANES] = _rms(kh, gkn_ref[...]).astype(BF16)
        k_ref[:, h * hw + LANES:(h + 1) * hw] = kpe

    for c in range(wvt_ref.shape[0] // hw):
        vt = jnp.dot(wvt_ref[c * hw:(c + 1) * hw, :], hkvt, preferred_element_type=F32)
        vt_ref[0, c * hw:(c + 1) * hw, :] = vt.astype(BF16)


def _qkv(z, pos_col, pos_row, consts, gql, gkvl, gqnt, gqr1, gqr2, gkn, gkr, wqt, wk, wvt, d: Dims, off_kr):
    t = z.shape[0]
    tm = d.tk
    invf, sgn, invft = consts
    hq = d.n_heads * 2 * LANES
    hv = d.n_heads * d.v_dim
    row = lambda a: pl.BlockSpec(a.shape, lambda i: (0, 0))
    kern = functools.partial(_qkv_kernel, n_heads=d.n_heads, rope=d.rope,
                             qscale=math.log2(math.e) / math.sqrt(d.nope + d.rope))
    return pl.pallas_call(
        kern,
        out_shape=(jax.ShapeDtypeStruct((hq, t), BF16),
                   jax.ShapeDtypeStruct((t, hq), BF16),
                   jax.ShapeDtypeStruct((t // tm, hv, tm), BF16)),
        grid=(t // tm,),
        in_specs=[
            pl.BlockSpec((tm, d.q_rank), lambda i: (i, 0)),
            pl.BlockSpec((tm, d.kv_rank), lambda i: (i, d.q_rank // d.kv_rank)),
            pl.BlockSpec((tm, LANES), lambda i: (i, off_kr // LANES)),
            pl.BlockSpec((tm, 1), lambda i: (i, 0)),
            pl.BlockSpec((1, tm), lambda i: (0, i)),
            row(invf), row(sgn), row(invft),
            row(gql), row(gkvl), row(gqnt), row(gqr1), row(gqr2), row(gkn), row(gkr),
            _const_spec(wqt.shape), _const_spec(wk.shape), _const_spec(wvt.shape),
        ],
        out_specs=(pl.BlockSpec((hq, tm), lambda i: (0, i)),
                   pl.BlockSpec((tm, hq), lambda i: (i, 0)),
                   pl.BlockSpec((1, hv, tm), lambda i: (i, 0, 0))),
        compiler_params=_cparams(("parallel",)),
        name="qkv",
    )(z, z, z, pos_col, pos_row, invf, sgn, invft, gql, gkvl, gqnt, gqr1, gqr2, gkn, gkr, wqt, wk, wvt)


def _attn_kernel(qt_ref, k_ref, vt_ref, wgu_ref, o_ref, wgu_bf_ref, *scr, tq, tk, gw):
    i = pl.program_id(2)
    n_sub = tq // tk
    ng = tq // gw
    wgu_bf_ref[...] = wgu_ref[...].astype(BF16)
    m_scr, l_scr, acc_scr = scr[:ng], scr[ng:2 * ng], scr[2 * ng:3 * ng]
    s_scr, cm_scr = scr[3 * ng:4 * ng], scr[4 * ng:]
    for g in range(ng):
        m_scr[g][...] = jnp.full(m_scr[g].shape, NEG_INF, F32)
        l_scr[g][...] = jnp.zeros(l_scr[g].shape, F32)
        acc_scr[g][...] = jnp.zeros(acc_scr[g].shape, F32)

    def scores(c, g):
        kc = k_ref[pl.ds(pl.multiple_of(c * tk, tk), tk), :]
        return jnp.dot(kc, qt_ref[:, g * gw:(g + 1) * gw], preferred_element_type=F32)

    def softmax_pv(c, g, s, cmax):
        m_old = m_scr[g][...]
        m_new = jnp.maximum(m_old, cmax)
        p = jnp.exp2(s - m_new)
        alpha = jnp.exp2(m_old - m_new)
        l_scr[g][...] = alpha * l_scr[g][...] + jnp.sum(p, axis=0, keepdims=True)
        pv = jnp.dot(vt_ref[c], p.astype(BF16), preferred_element_type=F32)
        acc_scr[g][...] = alpha * acc_scr[g][...] + pv
        m_scr[g][...] = m_new

    def stash(g, s):
        s_scr[g][...] = s
        cm_scr[g][...] = jnp.max(s, axis=0, keepdims=True)

    n_full = i * n_sub

    @pl.when(n_full > 0)
    def _():
        for g in range(ng):
            stash(g, scores(0, g))

        def body(c, carry):
            nxt = [scores(c + 1, g) for g in range(ng)]
            for g in range(ng):
                softmax_pv(c, g, s_scr[g][...], cm_scr[g][...])
            for g in range(ng):
                stash(g, nxt[g])
            return carry

        lax.fori_loop(0, n_full - 1, body, 0)
        for g in range(ng):
            softmax_pv(n_full - 1, g, s_scr[g][...], cm_scr[g][...])

    key = lax.broadcasted_iota(jnp.int32, (tk, gw), 0)
    qry = lax.broadcasted_iota(jnp.int32, (tk, gw), 1)
    steps = []
    for cc in range(n_sub):
        for g in range(ng):
            k_lo, k_hi = cc * tk, (cc + 1) * tk - 1
            q_lo, q_hi = g * gw, (g + 1) * gw - 1
            if k_lo > q_hi:
                continue
            steps.append((cc, g, k_hi > q_lo, k_lo, q_lo))

    def diag_scores(step):
        cc, g, masked, k_lo, q_lo = step
        s = scores(n_full + cc, g)
        if masked:
            s = jnp.where((key + k_lo) <= (qry + q_lo), s, NEG_INF)
        return s

    s_cur = diag_scores(steps[0])
    for j, step in enumerate(steps):
        s_nxt = diag_scores(steps[j + 1]) if j + 1 < len(steps) else None
        softmax_pv(n_full + step[0], step[1], s_cur, jnp.max(s_cur, axis=0, keepdims=True))
        s_cur = s_nxt

    for g in range(ng):
        o = acc_scr[g][...] / l_scr[g][...]
        o_ref[g * gw:(g + 1) * gw, :] = o.T.astype(o_ref.dtype)


def _attn(qt, k, vt, wgu2d, d: Dims):
    b, s, h = d.batch, d.seq, d.n_heads
    tq, tk = d.tq, d.tk
    gw = min(d.attn_gw, tq)
    ng = tq // gw
    nq = s // tq
    n_steps = b * h * nq
    ru = wgu2d.shape[0] // n_steps
    assert ru * n_steps == wgu2d.shape[0]
    step = lambda bi, hi, i: ((bi * h + hi) * nq + i, 0)
    kern = functools.partial(_attn_kernel, tq=tq, tk=tk, gw=gw)
    return pl.pallas_call(
        kern,
        out_shape=(jax.ShapeDtypeStruct((b * s, h * d.v_dim), BF16),
                   jax.ShapeDtypeStruct(wgu2d.shape, BF16)),
        grid=(b, h, nq),
        in_specs=[
            pl.BlockSpec((2 * LANES, tq), lambda bi, hi, i: (hi, bi * nq + i)),
            pl.BlockSpec((s, 2 * LANES), lambda bi, hi, i: (bi, hi)),
            pl.BlockSpec((s // tk, d.v_dim, tk), lambda bi, hi, i: (bi, hi, 0)),
            pl.BlockSpec((ru, wgu2d.shape[1]), step),
        ],
        out_specs=(pl.BlockSpec((tq, d.v_dim), lambda bi, hi, i: (bi * nq + i, hi)),
                   pl.BlockSpec((ru, wgu2d.shape[1]), step)),
        scratch_shapes=([pltpu.VMEM((1, gw), F32)] * (2 * ng) + [pltpu.VMEM((d.v_dim, gw), F32)] * ng
                        + [pltpu.VMEM((tk, gw), F32)] * ng + [pltpu.VMEM((1, gw), F32)] * ng),
        compiler_params=_cparams(("parallel", "parallel", "arbitrary")),
        name="attn",
    )(qt, k, vt, wgu2d)


def _mixout_kernel(u_ref, halo_ref, ga_ref, gp_ref, ya_ref, x_ref,
                   wpool_ref, spool_ref, wout_ref, gffn_ref, wr_ref, br_ref,
                   x1_ref, h4_ref, lg_ref, merged_scr, *, tm, seq, n_groups):
    i = pl.program_id(0)
    row0 = (i * tm) % seq
    gdim = u_ref.shape[1] // n_groups
    odim = ga_ref.shape[1] // n_groups
    pos = row0 + lax.broadcasted_iota(jnp.int32, (tm, 1), 0)
    halo_on = (row0 > 0).astype(F32)

    for g in range(n_groups):
        w = POOL_WINDOWS[g]
        ug = u_ref[:, g * gdim:(g + 1) * gdim].astype(F32)
        hg = halo_ref[:, g * gdim:(g + 1) * gdim].astype(F32) * halo_on
        ext = jnp.concatenate([hg, ug], axis=0)
        shift = 1
        while shift < w:
            ext = ext + pltpu.roll(ext, shift, 0)
            shift *= 2
        wsum = ext[POOL_HALO:, :]
        count = jnp.minimum(pos + 1, w).astype(F32)
        pooled = (wsum / count - ug).astype(BF16)
        yp = jnp.dot(pooled, wpool_ref[g], preferred_element_type=F32)
        yp = yp * spool_ref[:, g * odim:(g + 1) * odim]
        sl = slice(g * odim, (g + 1) * odim)
        merged = (_sigmoid(ga_ref[:, sl].astype(F32)) * ya_ref[:, sl].astype(F32)
                  + _sigmoid(gp_ref[:, sl].astype(F32)) * yp)
        merged_scr[:, sl] = merged.astype(BF16)

    x1 = x_ref[...] + jnp.dot(merged_scr[...], wout_ref[...], preferred_element_type=F32)
    x1_ref[...] = x1
    hn = _rms(x1, gffn_ref[...])
    hi = hn.astype(BF16)
    lo = (hn - hi.astype(F32)).astype(BF16)
    r = jnp.dot(jnp.concatenate([hi, lo], axis=0), wr_ref[...], preferred_element_type=F32)
    lg_ref[...] = (r[:tm, :LANES] + r[:tm, LANES:]) + (r[tm:, :LANES] + r[tm:, LANES:]) + br_ref[...]
    for c in range(h4_ref.shape[1]):
        h4_ref[:, c] = _row_tiles(hn[:, c * LANES:(c + 1) * LANES])


def _mixout(z, ya, x2d, wpool, spool, wout, gffn, wr, br, d: Dims, off_pool, off_ga, off_gp):
    t, dm = x2d.shape
    tm = d.tm_mix
    pw = d.pool_width
    n_groups = len(POOL_WINDOWS)
    hb = tm // POOL_HALO
    nc = dm // LANES
    kern = functools.partial(_mixout_kernel, tm=tm, seq=d.seq, n_groups=n_groups)
    return pl.pallas_call(
        kern,
        out_shape=(jax.ShapeDtypeStruct((t, dm), F32),
                   jax.ShapeDtypeStruct((t // SUBLANES, nc, SUBLANES, LANES), F32),
                   jax.ShapeDtypeStruct((t, LANES), F32)),
        grid=(t // tm,),
        in_specs=[
            pl.BlockSpec((tm, pw), lambda i: (i, off_pool // pw)),
            pl.BlockSpec((POOL_HALO, pw), lambda i: (jnp.maximum(i * hb - 1, 0), off_pool // pw)),
            pl.BlockSpec((tm, dm), lambda i: (i, off_ga // dm)),
            pl.BlockSpec((tm, dm), lambda i: (i, off_gp // dm)),
            pl.BlockSpec((tm, dm), lambda i: (i, 0)),
            pl.BlockSpec((tm, dm), lambda i: (i, 0)),
            _const_spec(wpool.shape), _const_spec(spool.shape), _const_spec(wout.shape),
            _const_spec(gffn.shape), _const_spec(wr.shape), _const_spec(br.shape),
        ],
        out_specs=(pl.BlockSpec((tm, dm), lambda i: (i, 0)),
                   pl.BlockSpec((tm // SUBLANES, nc, SUBLANES, LANES), lambda i: (i, 0, 0, 0)),
                   pl.BlockSpec((tm, LANES), lambda i: (i, 0))),
        scratch_shapes=[pltpu.VMEM((tm, dm), BF16)],
        compiler_params=_cparams(("parallel",)),
        name="mixout",
    )(z, z, z, z, ya, x2d, wpool, spool, wout, gffn, wr, br)


def _lane_excl_cumsum(v):
    lane = lax.broadcasted_iota(jnp.int32, v.shape, 1)
    inc = v
    shift = 1
    while shift < LANES:
        inc = inc + jnp.where(lane >= shift, pltpu.roll(inc, shift, 1), 0.0)
        shift *= 2
    return inc - v


def _route_kernel(lg_ref, dest_ref, gate_ref, cnt_ref, carry_scr, base_scr, *, tm, n_experts, tile):
    phase = pl.program_id(0)
    i = pl.program_id(1)
    lane = lax.broadcasted_iota(jnp.int32, (tm, LANES), 1)
    work = jnp.where(lane < n_experts, lg_ref[...], -jnp.inf)

    vals, hots = [], []
    for _ in range(TOP_K):
        m = jnp.max(work, axis=-1, keepdims=True)
        first = jnp.min(jnp.where(work == m, lane, LANES), axis=-1, keepdims=True)
        hot = lane == first
        vals.append(m)
        hots.append(hot)
        work = jnp.where(hot, -jnp.inf, work)
    chosen = jnp.zeros((tm, LANES), F32)
    for hot in hots:
        chosen = chosen + hot.astype(F32)

    @pl.when((phase == 0) & (i == 0))
    def _():
        carry_scr[...] = jnp.zeros(carry_scr.shape, F32)

    @pl.when((phase == 1) & (i == 0))
    def _():
        counts = carry_scr[...]
        padded = jnp.ceil(counts / tile) * tile
        base_scr[...] = _lane_excl_cumsum(padded)
        cnt_ref[...] = counts
        carry_scr[...] = jnp.zeros(carry_scr.shape, F32)

    @pl.when(phase == 1)
    def _():
        r = lax.broadcasted_iota(jnp.int32, (tm, tm), 0)
        c = lax.broadcasted_iota(jnp.int32, (tm, tm), 1)
        tri = (c < r).astype(BF16)
        before = jnp.dot(tri, chosen.astype(BF16), preferred_element_type=F32)
        slot = before + carry_scr[...] + base_scr[...]
        exps = [jnp.exp(v - vals[0]) for v in vals]
        den = exps[0] + exps[1] + exps[2] + exps[3]
        for k in range(TOP_K):
            dk = jnp.sum(jnp.where(hots[k], slot, 0.0), axis=-1, keepdims=True)
            dest_ref[:, k:k + 1] = dk.astype(jnp.int32)
            gate_ref[:, k:k + 1] = exps[k] / den

    carry_scr[...] = carry_scr[...] + jnp.sum(chosen, axis=0, keepdims=True)


def _route(logits, d: Dims):
    t = logits.shape[0]
    tm = d.tm_route
    kern = functools.partial(_route_kernel, tm=tm, n_experts=d.n_experts, tile=d.tm_moe)
    return pl.pallas_call(
        kern,
        out_shape=(jax.ShapeDtypeStruct((t, TOP_K), jnp.int32),
                   jax.ShapeDtypeStruct((t, TOP_K), F32),
                   jax.ShapeDtypeStruct((1, LANES), F32)),
        grid=(2, t // tm),
        in_specs=[pl.BlockSpec((tm, LANES), lambda p, i: (i, 0))],
        out_specs=(pl.BlockSpec((tm, TOP_K), lambda p, i: (i * p, 0)),
                   pl.BlockSpec((tm, TOP_K), lambda p, i: (i * p, 0)),
                   pl.BlockSpec((1, LANES), lambda p, i: (0, 0))),
        scratch_shapes=[pltpu.VMEM((1, LANES), F32), pltpu.VMEM((1, LANES), F32)],
        compiler_params=_cparams(("arbitrary", "arbitrary")),
        name="route",
    )(logits)


def _row_copy(src_ref, src_row, dst_ref, dst_row, sem):
    shift, low = SUBLANES.bit_length() - 1, SUBLANES - 1
    return pltpu.make_async_copy(
        src_ref.at[src_row >> shift, :, pl.ds(src_row & low, 1), :],
        dst_ref.at[dst_row >> shift, :, pl.ds(dst_row & low, 1), :], sem)


def _dispatch_kernel(dest_ref, h4_ref, xs_ref, sem, *, tm):
    i = pl.program_id(0)

    def copies(r):
        tok = i * tm + r
        return [_row_copy(h4_ref, r, xs_ref, dest_ref[tok * TOP_K + k], sem) for k in range(TOP_K)]

    def issue(r, carry):
        for cp in copies(r):
            cp.start()
        return carry

    def drain(r, carry):
        for cp in copies(r):
            cp.wait()
        return carry

    lax.fori_loop(0, tm, issue, 0)
    lax.fori_loop(0, tm, drain, 0)


def _dispatch(dest_flat, h4, n_rows, d: Dims):
    t = h4.shape[0] * SUBLANES
    tm = d.tm_disp
    kern = functools.partial(_dispatch_kernel, tm=tm)
    return pl.pallas_call(
        kern,
        out_shape=jax.ShapeDtypeStruct((n_rows // SUBLANES,) + h4.shape[1:], F32),
        grid_spec=pltpu.PrefetchScalarGridSpec(
            num_scalar_prefetch=1,
            grid=(t // tm,),
            in_specs=[pl.BlockSpec((tm // SUBLANES,) + h4.shape[1:], lambda i, dest: (i, 0, 0, 0))],
            out_specs=pl.BlockSpec(memory_space=pl.ANY),
            scratch_shapes=[pltpu.SemaphoreType.DMA],
        ),
        compiler_params=_cparams(("arbitrary",)),
        name="dispatch",
    )(dest_flat, h4)


def _moe_kernel(te_ref, ts_ref, tv_ref, nu_ref,
                xs_ref, wg_ref, wu_ref, bg_ref, bu_ref, wd_ref, bd_ref,
                y_ref, xb_scr, *, tm):
    i = pl.program_id(0)
    j = pl.program_id(1)
    nc = xs_ref.shape[1]

    @pl.when(i < nu_ref[0])
    def _():
        @pl.when(j == 0)
        def _():
            valid = _row_tiles(lax.broadcasted_iota(jnp.int32, (tm, LANES), 0)) < tv_ref[i]
            for c in range(nc):
                xb_scr[:, c * LANES:(c + 1) * LANES] = (
                    jnp.where(valid, xs_ref[:, c], 0.0).reshape(tm, LANES).astype(BF16))
                y_ref[:, c] = jnp.broadcast_to(bd_ref[0, :, c * LANES:(c + 1) * LANES],
                                               (tm // SUBLANES, SUBLANES, LANES))

        def ffn(rows):
            tg = rows // SUBLANES
            x = xb_scr[:rows, :]
            gate = jnp.dot(x, wg_ref[0], preferred_element_type=F32) + bg_ref[0]
            up = jnp.dot(x, wu_ref[0], preferred_element_type=F32) + bu_ref[0]
            gate = jnp.minimum(gate, SWIGLU_LIMIT)
            up = jnp.clip(up, -SWIGLU_LIMIT, SWIGLU_LIMIT)
            act = ((up + 1.0) * gate * _sigmoid(SWIGLU_ALPHA * gate)).astype(BF16)
            for c in range(0, nc, 2):
                part = jnp.dot(act, wd_ref[0, :, c * LANES:(c + 2) * LANES].astype(BF16),
                               preferred_element_type=F32)
                y_ref[:tg, c] = y_ref[:tg, c] + _row_tiles(part[:, :LANES])
                y_ref[:tg, c + 1] = y_ref[:tg, c + 1] + _row_tiles(part[:, LANES:])

        @pl.when(tv_ref[i] > tm // 2)
        def _():
            ffn(tm)

        @pl.when(tv_ref[i] <= tm // 2)
        def _():
            ffn(tm // 2)


def _moe(tile_expert, tile_src, tile_valid, n_used, xs4, wgu, bgu, wd, bd, d: Dims):
    n_rows = xs4.shape[0] * SUBLANES
    tm, tf = d.tm_moe, d.tf_moe
    dm, ff = d.d_model, d.d_ff
    nj = ff // tf
    n_tiles = n_rows // tm
    blk4 = (tm // SUBLANES,) + xs4.shape[1:]

    def jj(i, j, nu):
        return jnp.where(i < nu[0], j, nj - 1)

    kern = functools.partial(_moe_kernel, tm=tm)
    return pl.pallas_call(
        kern,
        out_shape=jax.ShapeDtypeStruct(xs4.shape, F32),
        grid_spec=pltpu.PrefetchScalarGridSpec(
            num_scalar_prefetch=4,
            grid=(n_tiles, nj),
            in_specs=[
                pl.BlockSpec(blk4, lambda i, j, te, ts, tv, nu: (ts[i], 0, 0, 0)),
                pl.BlockSpec((1, dm, tf), lambda i, j, te, ts, tv, nu: (te[i], 0, jj(i, j, nu))),
                pl.BlockSpec((1, dm, tf), lambda i, j, te, ts, tv, nu: (te[i], 0, nj + jj(i, j, nu))),
                pl.BlockSpec((1, 1, tf), lambda i, j, te, ts, tv, nu: (te[i], 0, jj(i, j, nu))),
                pl.BlockSpec((1, 1, tf), lambda i, j, te, ts, tv, nu: (te[i], 0, nj + jj(i, j, nu))),
                pl.BlockSpec((1, tf, dm), lambda i, j, te, ts, tv, nu: (te[i], jj(i, j, nu), 0)),
                pl.BlockSpec((1, 1, dm), lambda i, j, te, ts, tv, nu: (te[i], 0, 0)),
            ],
            out_specs=pl.BlockSpec(blk4, lambda i, j, te, ts, tv, nu: (ts[i], 0, 0, 0)),
            scratch_shapes=[pltpu.VMEM((tm, dm), BF16)],
        ),
        compiler_params=_cparams(("arbitrary", "arbitrary")),
        name="moe",
    )(tile_expert, tile_src, tile_valid, n_used, xs4, wgu, wgu, bgu, bgu, wd, bd)


def _combine_kernel(dest_ref, y4_ref, gate_ref, x1_ref, o_ref, buf_a, buf_b, sem_a, sem_b, *, tm):
    i = pl.program_id(0)
    n = pl.num_programs(0)

    def copies(half_blk, r, buf, sem):
        tok = half_blk * tm + r
        return [_row_copy(y4_ref, dest_ref[tok * TOP_K + k], buf, k * tm + r, sem) for k in range(TOP_K)]

    def issue(half_blk, buf, sem):
        def body(r, carry):
            for cp in copies(half_blk, r, buf, sem):
                cp.start()
            return carry
        lax.fori_loop(0, tm, body, 0)

    def drain(half_blk, buf, sem):
        def body(r, carry):
            for cp in copies(half_blk, r, buf, sem):
                cp.wait()
            return carry
        lax.fori_loop(0, tm, body, 0)

    def reduce(buf, row0):
        gates = gate_ref[row0:row0 + tm, :]
        tg = tm // SUBLANES
        for c in range(buf.shape[1]):
            acc = x1_ref[row0:row0 + tm, c * LANES:(c + 1) * LANES]
            for k in range(TOP_K):
                acc = acc + gates[:, k:k + 1] * buf[k * tg:(k + 1) * tg, c].reshape(tm, LANES)
            o_ref[row0:row0 + tm, c * LANES:(c + 1) * LANES] = acc

    @pl.when(i == 0)
    def _():
        issue(0, buf_a, sem_a)

    issue(2 * i + 1, buf_b, sem_b)
    drain(2 * i, buf_a, sem_a)
    reduce(buf_a, 0)

    @pl.when(i < n - 1)
    def _():
        issue(2 * i + 2, buf_a, sem_a)

    drain(2 * i + 1, buf_b, sem_b)
    reduce(buf_b, tm)


def _combine(dest_flat, y4, gates, x1, d: Dims):
    t, dm = x1.shape
    tm = d.tm_comb
    kern = functools.partial(_combine_kernel, tm=tm)
    buf = pltpu.VMEM((TOP_K * tm // SUBLANES,) + y4.shape[1:], F32)
    return pl.pallas_call(
        kern,
        out_shape=jax.ShapeDtypeStruct((t, dm), F32),
        grid_spec=pltpu.PrefetchScalarGridSpec(
            num_scalar_prefetch=1,
            grid=(t // (2 * tm),),
            in_specs=[
                pl.BlockSpec(memory_space=pl.ANY),
                pl.BlockSpec((2 * tm, TOP_K), lambda i, dest: (i, 0)),
                pl.BlockSpec((2 * tm, dm), lambda i, dest: (i, 0)),
            ],
            out_specs=pl.BlockSpec((2 * tm, dm), lambda i, dest: (i, 0)),
            scratch_shapes=[buf, buf, pltpu.SemaphoreType.DMA, pltpu.SemaphoreType.DMA],
        ),
        compiler_params=_cparams(("arbitrary",)),
        name="combine",
    )(dest_flat, y4, gates, x1)


def _pad_lanes(v, fill=0.0):
    v = v.reshape(1, -1).astype(F32)
    return jnp.pad(v, ((0, 0), (0, LANES - v.shape[1])), constant_values=fill)


def _layer(d: Dims, x, positions, g_mix, w_in, g_q_latent, w_uq, g_kv_latent, w_ukv,
           g_q_nope, g_q_rope, g_k_nope, g_k_rope, w_pool, s_pool, w_out,
           g_ffn, w_router, b_router, w_gate_up, b_gate_up, w_down, b_down):
    b, s, dm, h = d.batch, d.seq, d.d_model, d.n_heads
    t = b * s
    assert d.nope == LANES and d.v_dim == LANES and d.rope <= LANES and dm % LANES == 0
    assert d.q_rank == d.kv_rank and d.pool_width % d.q_rank == 0 and dm % d.pool_width == 0
    assert d.tq % d.tk == 0 and s % d.tq == 0

    o_q, o_kv = 0, d.q_rank
    o_kr = o_kv + d.kv_rank
    o_pool = o_kr + d.rope
    o_ga = o_pool + d.pool_width
    o_gp = o_ga + dm
    n_pool = -(-(d.q_rank + d.kv_rank) // d.pool_width) * d.pool_width
    n_ga = -(-(n_pool + d.pool_width) // dm) * dm
    n_gp = n_ga + dm
    n_kr = n_gp + dm
    nz = -(-(n_kr + LANES) // d.tn_in) * d.tn_in
    zeros = lambda n: jnp.zeros((dm, n), w_in.dtype)
    w_in_r = jnp.concatenate([
        w_in[:, o_q:o_kr], zeros(n_pool - (d.q_rank + d.kv_rank)),
        w_in[:, o_pool:o_ga], zeros(n_ga - n_pool - d.pool_width),
        w_in[:, o_ga:o_gp], w_in[:, o_gp:],
        w_in[:, o_kr:o_pool], zeros(nz - n_kr - d.rope)], axis=1).astype(BF16)

    qk = d.nope + d.rope
    half = d.rope // 2
    wq3 = w_uq.reshape(d.q_rank, h, qk)
    wq = jnp.concatenate([wq3, jnp.zeros((d.q_rank, h, 2 * LANES - qk), w_uq.dtype)], axis=2)
    wqt = wq.reshape(d.q_rank, h * 2 * LANES).T.astype(BF16)
    wkv3 = w_ukv.reshape(d.kv_rank, h, d.nope + d.v_dim)
    wk = wkv3[:, :, :d.nope].reshape(d.kv_rank, h * d.nope).astype(BF16)
    wvt = wkv3[:, :, d.nope:].reshape(d.kv_rank, h * d.v_dim).T.astype(BF16)

    inv_freq = ROPE_THETA ** (-np.arange(0, d.rope, 2, dtype=np.float32) / d.rope)
    invf = np.zeros((1, LANES), np.float32)
    invf[0, :half] = inv_freq
    invf[0, half:d.rope] = inv_freq
    sgn = np.zeros((1, LANES), np.float32)
    sgn[0, :half] = -1.0
    sgn[0, half:d.rope] = 1.0
    tmq = d.tk
    invft = np.ascontiguousarray(np.broadcast_to(inv_freq[:, None], (half, tmq)))
    consts = (jnp.asarray(invf), jnp.asarray(sgn), jnp.asarray(invft))
    bcast = lambda v: jnp.broadcast_to(v.astype(F32)[:, None], (v.shape[0], tmq))

    x2d = x.reshape(t, dm)
    pos_f = positions.reshape(t).astype(F32)

    z = _inproj(x2d, g_mix.reshape(1, dm), w_in_r, d)
    qt, k, vt = _qkv(z, pos_f.reshape(t, 1), pos_f.reshape(1, t), consts,
                     g_q_latent.reshape(1, -1), g_kv_latent.reshape(1, -1),
                     bcast(g_q_nope), bcast(g_q_rope[:half]), bcast(g_q_rope[half:]),
                     _pad_lanes(g_k_nope), _pad_lanes(g_k_rope), wqt, wk, wvt, d, n_kr)
    ya, wgu_bf = _attn(qt, k, vt, w_gate_up.reshape(d.n_experts * dm, 2 * d.d_ff), d)
    wr = jnp.pad(w_router.astype(F32), ((0, 0), (0, LANES - d.n_experts)))
    wr_hi = wr.astype(BF16)
    wr = jnp.concatenate([wr_hi, (wr - wr_hi.astype(F32)).astype(BF16)], axis=1)
    x1, h4, logits = _mixout(z, ya, x2d, w_pool.astype(BF16), s_pool.reshape(1, dm), w_out.astype(BF16),
                             g_ffn.reshape(1, dm), wr, _pad_lanes(b_router), d, n_pool, n_ga, n_gp)

    dest, gates, counts = _route(logits, d)
    tm = d.tm_moe
    n_rows = t * TOP_K + d.n_experts * tm
    n_tiles = n_rows // tm
    cnt = counts[0, :d.n_experts].astype(jnp.int32)
    ends = jnp.cumsum((cnt + tm - 1) // tm * tm)
    n_used = ends[-1] // tm
    tile_ids = jnp.minimum(jnp.arange(n_tiles, dtype=jnp.int32), n_used - 1)
    tile_expert = jnp.sum(ends[None, :] <= (tile_ids * tm)[:, None], axis=1).astype(jnp.int32)
    tile_expert = jnp.minimum(tile_expert, d.n_experts - 1)
    starts = ends - (cnt + tm - 1) // tm * tm
    tile_valid = jnp.clip(starts[tile_expert] + cnt[tile_expert] - tile_ids * tm, 0, tm).astype(jnp.int32)

    dest_flat = dest.reshape(t * TOP_K)
    xs4 = _dispatch(dest_flat, h4, n_rows, d)
    y4 = _moe(tile_expert, tile_ids, tile_valid, n_used.reshape(1).astype(jnp.int32), xs4,
              wgu_bf.reshape(w_gate_up.shape), b_gate_up.reshape(d.n_experts, 1, 2 * d.d_ff),
              w_down, b_down.reshape(d.n_experts, 1, dm), d)
    out = _combine(dest_flat, y4, gates, x1, d)
    return out.reshape(b, s, dm)


def kernel(x, positions, g_mix, w_in, g_q_latent, w_uq, g_kv_latent, w_ukv, g_q_nope, g_q_rope, g_k_nope, g_k_rope, w_pool, s_pool, w_out, g_ffn, w_router, b_router, w_gate_up, b_gate_up, w_down, b_down):
    return _layer(Dims(), x, positions, g_mix, w_in, g_q_latent, w_uq, g_kv_latent, w_ukv,
                  g_q_nope, g_q_rope, g_k_nope, g_k_rope, w_pool, s_pool, w_out,
                  g_ffn, w_router, b_router, w_gate_up, b_gate_up, w_down, b_down)
```

```python
import functools
import math
from typing import NamedTuple

import numpy as np
import jax
import jax.numpy as jnp
from jax import lax
from jax.experimental import pallas as pl
from jax.experimental.pallas import tpu as pltpu

F32 = jnp.float32
BF16 = jnp.bfloat16

RMS_EPS = 1e-6
NEG_INF = -1e30
ROPE_THETA = 10000.0
SWIGLU_LIMIT = 7.0
SWIGLU_ALPHA = 1.702
POOL_WINDOWS = (2, 4, 8, 16)
TOP_K = 4

LANES = 128
SUBLANES = 8
POOL_HALO = 16
ROW_UNROLL = 4
VMEM_LIMIT = 56 * 1024 * 1024


class Dims(NamedTuple):
    batch: int = 2
    seq: int = 8192
    d_model: int = 2048
    n_heads: int = 16
    nope: int = 128
    rope: int = 64
    v_dim: int = 128
    q_rank: int = 512
    kv_rank: int = 512
    pool_width: int = 1024
    n_experts: int = 32
    d_ff: int = 2048
    tm_in: int = 1024
    tn_in: int = 1664
    tq: int = 2048
    tk: int = 512
    attn_gw: int = 512
    tm_mix: int = 256
    tm_route: int = 512
    tm_disp: int = 256
    tm_moe: int = 512
    tf_moe: int = 512
    tm_comb: int = 128


def _cparams(sem, vmem=VMEM_LIMIT):
    return pltpu.CompilerParams(dimension_semantics=sem, vmem_limit_bytes=vmem)


def _const_spec(shape):
    nd = len(shape)
    return pl.BlockSpec(shape, lambda *_: (0,) * nd, pipeline_mode=pl.Buffered(1))


def _rms(t, gain):
    return t * lax.rsqrt(jnp.mean(t * t, axis=-1, keepdims=True) + RMS_EPS) * gain


def _sigmoid(t):
    return 0.5 * jnp.tanh(0.5 * t) + 0.5


def _row_tiles(v):
    return v.reshape(v.shape[0] // SUBLANES, SUBLANES, LANES)


def _inproj_kernel(x_ref, g_ref, w_ref, z_ref, h_scr):
    @pl.when(pl.program_id(1) == 0)
    def _():
        h_scr[...] = _rms(x_ref[...], g_ref[...]).astype(BF16)

    z_ref[...] = jnp.dot(h_scr[...], w_ref[...], preferred_element_type=F32).astype(z_ref.dtype)


def _inproj(x2d, g_mix, w_in_r, d: Dims):
    t, dm = x2d.shape
    nz = w_in_r.shape[1]
    tm, tn = d.tm_in, d.tn_in
    return pl.pallas_call(
        _inproj_kernel,
        out_shape=jax.ShapeDtypeStruct((t, nz), BF16),
        grid=(t // tm, nz // tn),
        in_specs=[
            pl.BlockSpec((tm, dm), lambda i, j: (i, 0)),
            pl.BlockSpec((1, dm), lambda i, j: (0, 0)),
            pl.BlockSpec((dm, tn), lambda i, j: (0, j)),
        ],
        out_specs=pl.BlockSpec((tm, tn), lambda i, j: (i, j)),
        scratch_shapes=[pltpu.VMEM((tm, dm), BF16)],
        compiler_params=_cparams(("parallel", "arbitrary")),
        name="inproj",
    )(x2d, g_mix, w_in_r)


def _rope_tile(t, cosf, sinf, half):
    lane = lax.broadcasted_iota(jnp.int32, t.shape, 1)
    swapped = jnp.where(lane < half, pltpu.roll(t, LANES - half, 1), pltpu.roll(t, half, 1))
    return t * cosf + swapped * sinf


def _qkv_kernel(zq_ref, zkv_ref, zkr_ref, pos_ref, posr_ref, invf_ref, sgn_ref, invft_ref,
                gql_ref, gkvl_ref, gqnt_ref, gqr1_ref, gqr2_ref, gkn_ref, gkr_ref,
                wqt_ref, wk_ref, wvt_ref,
                qt_ref, k_ref, vt_ref, *, n_heads, rope, qscale):
    half = rope // 2
    hw = 2 * LANES
    ang = pos_ref[...] * invf_ref[...]
    cosf = jnp.cos(ang)
    sinf = jnp.sin(ang) * sgn_ref[...]
    angt = invft_ref[...] * posr_ref[...]
    cost = jnp.cos(angt)
    sint = jnp.sin(angt)

    hq = _rms(zq_ref[...].astype(F32), gql_ref[...])
    hkv = _rms(zkv_ref[...].astype(F32), gkvl_ref[...])
    hqt = hq.T.astype(BF16)
    hkvt = hkv.T.astype(BF16)
    hkv = hkv.astype(BF16)

    kr = zkr_ref[...].astype(F32)
    kr = kr * lax.rsqrt(jnp.sum(kr * kr, axis=-1, keepdims=True) / rope + RMS_EPS) * gkr_ref[...]
    kpe = _rope_tile(kr, cosf, sinf, half).astype(BF16)

    for h in range(n_heads):
        qh = jnp.dot(wqt_ref[h * hw:(h + 1) * hw, :], hqt, preferred_element_type=F32)
        qn = qh[:LANES]
        qn = qn * lax.rsqrt(jnp.mean(qn * qn, axis=0, keepdims=True) + RMS_EPS) * gqnt_ref[...]
        t1 = qh[LANES:LANES + half]
        t2 = qh[LANES + half:LANES + rope]
        ss = jnp.sum(t1 * t1, axis=0, keepdims=True) + jnp.sum(t2 * t2, axis=0, keepdims=True)
        r = lax.rsqrt(ss / rope + RMS_EPS)
        t1 = t1 * r * gqr1_ref[...]
        t2 = t2 * r * gqr2_ref[...]
        qt_ref[h * hw:h * hw + LANES, :] = (qn * qscale).astype(BF16)
        qt_ref[h * hw + LANES:h * hw + LANES + half, :] = ((t1 * cost - t2 * sint) * qscale).astype(BF16)
        qt_ref[h * hw + LANES + half:h * hw + LANES + rope, :] = ((t2 * cost + t1 * sint) * qscale).astype(BF16)
        qt_ref[h * hw + LANES + rope:(h + 1) * hw, :] = jnp.zeros((LANES - rope, qt_ref.shape[1]), BF16)

        kh = jnp.dot(hkv, wk_ref[:, h * LANES:(h + 1) * LANES], preferred_element_type=F32)
        k_ref[:, h * hw:h * hw + LANES] = _rms(kh, gkn_ref[...]).astype(BF16)
        k_ref[:, h * hw + LANES:(h + 1) * hw] = kpe

    for c in range(wvt_ref.shape[0] // hw):
        vt = jnp.dot(wvt_ref[c * hw:(c + 1) * hw, :], hkvt, preferred_element_type=F32)
        vt_ref[0, c * hw:(c + 1) * hw, :] = vt.astype(BF16)


def _qkv(z, pos_col, pos_row, consts, gql, gkvl, gqnt, gqr1, gqr2, gkn, gkr, wqt, wk, wvt, d: Dims, off_kr):
    t = z.shape[0]
    tm = d.tk
    invf, sgn, invft = consts
    hq = d.n_heads * 2 * LANES
    hv = d.n_heads * d.v_dim
    row = lambda a: pl.BlockSpec(a.shape, lambda i: (0, 0))
    kern = functools.partial(_qkv_kernel, n_heads=d.n_heads, rope=d.rope,
                             qscale=math.log2(math.e) / math.sqrt(d.nope + d.rope))
    return pl.pallas_call(
        kern,
        out_shape=(jax.ShapeDtypeStruct((hq, t), BF16),
                   jax.ShapeDtypeStruct((t, hq), BF16),
                   jax.ShapeDtypeStruct((t // tm, hv, tm), BF16)),
        grid=(t // tm,),
        in_specs=[
            pl.BlockSpec((tm, d.q_rank), lambda i: (i, 0)),
            pl.BlockSpec((tm, d.kv_rank), lambda i: (i, d.q_rank // d.kv_rank)),
            pl.BlockSpec((tm, LANES), lambda i: (i, off_kr // LANES)),
            pl.BlockSpec((tm, 1), lambda i: (i, 0)),
            pl.BlockSpec((1, tm), lambda i: (0, i)),
            row(invf), row(sgn), row(invft),
            row(gql), row(gkvl), row(gqnt), row(gqr1), row(gqr2), row(gkn), row(gkr),
            _const_spec(wqt.shape), _const_spec(wk.shape), _const_spec(wvt.shape),
        ],
        out_specs=(pl.BlockSpec((hq, tm), lambda i: (0, i)),
                   pl.BlockSpec((tm, hq), lambda i: (i, 0)),
                   pl.BlockSpec((1, hv, tm), lambda i: (i, 0, 0))),
        compiler_params=_cparams(("parallel",)),
        name="qkv",
    )(z, z, z, pos_col, pos_row, invf, sgn, invft, gql, gkvl, gqnt, gqr1, gqr2, gkn, gkr, wqt, wk, wvt)


def _attn_kernel(qt_ref, k_ref, vt_ref, wgu_ref, o_ref, wgu_bf_ref, *scr, tq, tk, gw):
    i = pl.program_id(2)
    n_sub = tq // tk
    ng = tq // gw
    cw = wgu_bf_ref.shape[3]
    for c in range(wgu_bf_ref.shape[1]):
        wgu_bf_ref[0, c] = wgu_ref[:, c * cw:(c + 1) * cw].astype(BF16)
    m_scr, l_scr, acc_scr = scr[:ng], scr[ng:2 * ng], scr[2 * ng:3 * ng]
    s_scr, cm_scr = scr[3 * ng:4 * ng], scr[4 * ng:]
    for g in range(ng):
        m_scr[g][...] = jnp.full(m_scr[g].shape, NEG_INF, F32)
        l_scr[g][...] = jnp.zeros(l_scr[g].shape, F32)
        acc_scr[g][...] = jnp.zeros(acc_scr[g].shape, F32)

    def scores(c, g):
        kc = k_ref[pl.ds(pl.multiple_of(c * tk, tk), tk), :]
        return jnp.dot(kc, qt_ref[:, g * gw:(g + 1) * gw], preferred_element_type=F32)

    def softmax_pv(c, g, s, cmax):
        m_old = m_scr[g][...]
        m_new = jnp.maximum(m_old, cmax)
        p = jnp.exp2(s - m_new)
        alpha = jnp.exp2(m_old - m_new)
        l_scr[g][...] = alpha * l_scr[g][...] + jnp.sum(p, axis=0, keepdims=True)
        pv = jnp.dot(vt_ref[c], p.astype(BF16), preferred_element_type=F32)
        acc_scr[g][...] = alpha * acc_scr[g][...] + pv
        m_scr[g][...] = m_new

    def stash(g, s):
        s_scr[g][...] = s
        cm_scr[g][...] = jnp.max(s, axis=0, keepdims=True)

    n_full = i * n_sub

    @pl.when(n_full > 0)
    def _():
        for g in range(ng):
            stash(g, scores(0, g))

        def body(c, carry):
            nxt = [scores(c + 1, g) for g in range(ng)]
            for g in range(ng):
                softmax_pv(c, g, s_scr[g][...], cm_scr[g][...])
            for g in range(ng):
                stash(g, nxt[g])
            return carry

        lax.fori_loop(0, n_full - 1, body, 0)
        for g in range(ng):
            softmax_pv(n_full - 1, g, s_scr[g][...], cm_scr[g][...])

    key = lax.broadcasted_iota(jnp.int32, (tk, gw), 0)
    qry = lax.broadcasted_iota(jnp.int32, (tk, gw), 1)
    steps = []
    for cc in range(n_sub):
        for g in range(ng):
            k_lo, k_hi = cc * tk, (cc + 1) * tk - 1
            q_lo, q_hi = g * gw, (g + 1) * gw - 1
            if k_lo > q_hi:
                continue
            steps.append((cc, g, k_hi > q_lo, k_lo, q_lo))

    def diag_scores(step):
        cc, g, masked, k_lo, q_lo = step
        s = scores(n_full + cc, g)
        if masked:
            s = jnp.where((key + k_lo) <= (qry + q_lo), s, NEG_INF)
        return s

    s_cur = diag_scores(steps[0])
    for j, step in enumerate(steps):
        s_nxt = diag_scores(steps[j + 1]) if j + 1 < len(steps) else None
        softmax_pv(n_full + step[0], step[1], s_cur, jnp.max(s_cur, axis=0, keepdims=True))
        s_cur = s_nxt

    for g in range(ng):
        o = acc_scr[g][...] / l_scr[g][...]
        o_ref[g * gw:(g + 1) * gw, :] = o.T.astype(o_ref.dtype)


def _attn(qt, k, vt, wgu2d, d: Dims):
    b, s, h = d.batch, d.seq, d.n_heads
    tq, tk = d.tq, d.tk
    gw = min(d.attn_gw, tq)
    ng = tq // gw
    nq = s // tq
    n_steps = b * h * nq
    ru = wgu2d.shape[0] // n_steps
    dm, tf = d.d_model, d.tf_moe
    spe = dm // ru
    ncol = wgu2d.shape[1] // tf
    assert ru * n_steps == wgu2d.shape[0] and spe * ru == dm
    step = lambda bi, hi, i: ((bi * h + hi) * nq + i, 0)
    kern = functools.partial(_attn_kernel, tq=tq, tk=tk, gw=gw)
    return pl.pallas_call(
        kern,
        out_shape=(jax.ShapeDtypeStruct((b * s, h * d.v_dim), BF16),
                   jax.ShapeDtypeStruct((d.n_experts, ncol, dm, tf), BF16)),
        grid=(b, h, nq),
        in_specs=[
            pl.BlockSpec((2 * LANES, tq), lambda bi, hi, i: (hi, bi * nq + i)),
            pl.BlockSpec((s, 2 * LANES), lambda bi, hi, i: (bi, hi)),
            pl.BlockSpec((s // tk, d.v_dim, tk), lambda bi, hi, i: (bi, hi, 0)),
            pl.BlockSpec((ru, wgu2d.shape[1]), step),
        ],
        out_specs=(pl.BlockSpec((tq, d.v_dim), lambda bi, hi, i: (bi * nq + i, hi)),
                   pl.BlockSpec((1, ncol, ru, tf),
                                lambda bi, hi, i: (step(bi, hi, i)[0] // spe, 0, step(bi, hi, i)[0] % spe, 0))),
        scratch_shapes=([pltpu.VMEM((1, gw), F32)] * (2 * ng) + [pltpu.VMEM((d.v_dim, gw), F32)] * ng
                        + [pltpu.VMEM((tk, gw), F32)] * ng + [pltpu.VMEM((1, gw), F32)] * ng),
        compiler_params=_cparams(("parallel", "parallel", "arbitrary")),
        name="attn",
    )(qt, k, vt, wgu2d)


def _mixout_kernel(u_ref, halo_ref, ga_ref, gp_ref, ya_ref, x_ref,
                   wpool_ref, spool_ref, wout_ref, gffn_ref, wr_ref, br_ref,
                   x1_ref, h4_ref, lg_ref, merged_scr, *, tm, seq, n_groups):
    i = pl.program_id(0)
    row0 = (i * tm) % seq
    gdim = u_ref.shape[1] // n_groups
    odim = ga_ref.shape[1] // n_groups
    pos = row0 + lax.broadcasted_iota(jnp.int32, (tm, 1), 0)
    halo_on = (row0 > 0).astype(F32)

    for g in range(n_groups):
        w = POOL_WINDOWS[g]
        ug = u_ref[:, g * gdim:(g + 1) * gdim].astype(F32)
        hg = halo_ref[:, g * gdim:(g + 1) * gdim].astype(F32) * halo_on
        ext = jnp.concatenate([hg, ug], axis=0)
        shift = 1
        while shift < w:
            ext = ext + pltpu.roll(ext, shift, 0)
            shift *= 2
        wsum = ext[POOL_HALO:, :]
        count = jnp.minimum(pos + 1, w).astype(F32)
        pooled = (wsum / count - ug).astype(BF16)
        yp = jnp.dot(pooled, wpool_ref[g], preferred_element_type=F32)
        yp = yp * spool_ref[:, g * odim:(g + 1) * odim]
        sl = slice(g * odim, (g + 1) * odim)
        merged = (_sigmoid(ga_ref[:, sl].astype(F32)) * ya_ref[:, sl].astype(F32)
                  + _sigmoid(gp_ref[:, sl].astype(F32)) * yp)
        merged_scr[:, sl] = merged.astype(BF16)

    x1 = x_ref[...] + jnp.dot(merged_scr[...], wout_ref[...], preferred_element_type=F32)
    x1_ref[...] = x1
    hn = _rms(x1, gffn_ref[...])
    hi = hn.astype(BF16)
    lo = (hn - hi.astype(F32)).astype(BF16)
    r = jnp.dot(jnp.concatenate([hi, lo], axis=0), wr_ref[...], preferred_element_type=F32)
    lg_ref[...] = (r[:tm, :LANES] + r[:tm, LANES:]) + (r[tm:, :LANES] + r[tm:, LANES:]) + br_ref[...]
    for c in range(h4_ref.shape[1]):
        h4_ref[:, c] = _row_tiles(hn[:, c * LANES:(c + 1) * LANES])


def _mixout(z, ya, x2d, wpool, spool, wout, gffn, wr, br, d: Dims, off_pool, off_ga, off_gp):
    t, dm = x2d.shape
    tm = d.tm_mix
    pw = d.pool_width
    n_groups = len(POOL_WINDOWS)
    hb = tm // POOL_HALO
    nc = dm // LANES
    kern = functools.partial(_mixout_kernel, tm=tm, seq=d.seq, n_groups=n_groups)
    return pl.pallas_call(
        kern,
        out_shape=(jax.ShapeDtypeStruct((t, dm), F32),
                   jax.ShapeDtypeStruct((t // SUBLANES, nc, SUBLANES, LANES), F32),
                   jax.ShapeDtypeStruct((t, LANES), F32)),
        grid=(t // tm,),
        in_specs=[
            pl.BlockSpec((tm, pw), lambda i: (i, off_pool // pw)),
            pl.BlockSpec((POOL_HALO, pw), lambda i: (jnp.maximum(i * hb - 1, 0), off_pool // pw)),
            pl.BlockSpec((tm, dm), lambda i: (i, off_ga // dm)),
            pl.BlockSpec((tm, dm), lambda i: (i, off_gp // dm)),
            pl.BlockSpec((tm, dm), lambda i: (i, 0)),
            pl.BlockSpec((tm, dm), lambda i: (i, 0)),
            _const_spec(wpool.shape), _const_spec(spool.shape), _const_spec(wout.shape),
            _const_spec(gffn.shape), _const_spec(wr.shape), _const_spec(br.shape),
        ],
        out_specs=(pl.BlockSpec((tm, dm), lambda i: (i, 0)),
                   pl.BlockSpec((tm // SUBLANES, nc, SUBLANES, LANES), lambda i: (i, 0, 0, 0)),
                   pl.BlockSpec((tm, LANES), lambda i: (i, 0))),
        scratch_shapes=[pltpu.VMEM((tm, dm), BF16)],
        compiler_params=_cparams(("parallel",)),
        name="mixout",
    )(z, z, z, z, ya, x2d, wpool, spool, wout, gffn, wr, br)


def _lane_excl_cumsum(v):
    lane = lax.broadcasted_iota(jnp.int32, v.shape, 1)
    inc = v
    shift = 1
    while shift < LANES:
        inc = inc + jnp.where(lane >= shift, pltpu.roll(inc, shift, 1), 0.0)
        shift *= 2
    return inc - v


def _route_kernel(lg_ref, dest_ref, gate_ref, cnt_ref, carry_scr, base_scr, *, tm, n_experts, tile):
    phase = pl.program_id(0)
    i = pl.program_id(1)
    lane = lax.broadcasted_iota(jnp.int32, (tm, LANES), 1)
    work = jnp.where(lane < n_experts, lg_ref[...], -jnp.inf)

    vals, hots = [], []
    for _ in range(TOP_K):
        m = jnp.max(work, axis=-1, keepdims=True)
        first = jnp.min(jnp.where(work == m, lane, LANES), axis=-1, keepdims=True)
        hot = lane == first
        vals.append(m)
        hots.append(hot)
        work = jnp.where(hot, -jnp.inf, work)
    chosen = jnp.zeros((tm, LANES), F32)
    for hot in hots:
        chosen = chosen + hot.astype(F32)

    @pl.when((phase == 0) & (i == 0))
    def _():
        carry_scr[...] = jnp.zeros(carry_scr.shape, F32)

    @pl.when((phase == 1) & (i == 0))
    def _():
        counts = carry_scr[...]
        padded = jnp.ceil(counts / tile) * tile
        base_scr[...] = _lane_excl_cumsum(padded)
        cnt_ref[...] = counts
        carry_scr[...] = jnp.zeros(carry_scr.shape, F32)

    @pl.when(phase == 1)
    def _():
        r = lax.broadcasted_iota(jnp.int32, (tm, tm), 0)
        c = lax.broadcasted_iota(jnp.int32, (tm, tm), 1)
        tri = (c < r).astype(BF16)
        before = jnp.dot(tri, chosen.astype(BF16), preferred_element_type=F32)
        slot = before + carry_scr[...] + base_scr[...]
        exps = [jnp.exp(v - vals[0]) for v in vals]
        den = exps[0] + exps[1] + exps[2] + exps[3]
        for k in range(TOP_K):
            dk = jnp.sum(jnp.where(hots[k], slot, 0.0), axis=-1, keepdims=True)
            dest_ref[:, k:k + 1] = dk.astype(jnp.int32)
            gate_ref[:, k:k + 1] = exps[k] / den

    carry_scr[...] = carry_scr[...] + jnp.sum(chosen, axis=0, keepdims=True)


def _route(logits, d: Dims):
    t = logits.shape[0]
    tm = d.tm_route
    kern = functools.partial(_route_kernel, tm=tm, n_experts=d.n_experts, tile=d.tm_moe)
    return pl.pallas_call(
        kern,
        out_shape=(jax.ShapeDtypeStruct((t, TOP_K), jnp.int32),
                   jax.ShapeDtypeStruct((t, TOP_K), F32),
                   jax.ShapeDtypeStruct((1, LANES), F32)),
        grid=(2, t // tm),
        in_specs=[pl.BlockSpec((tm, LANES), lambda p, i: (i, 0))],
        out_specs=(pl.BlockSpec((tm, TOP_K), lambda p, i: (i * p, 0)),
                   pl.BlockSpec((tm, TOP_K), lambda p, i: (i * p, 0)),
                   pl.BlockSpec((1, LANES), lambda p, i: (0, 0))),
        scratch_shapes=[pltpu.VMEM((1, LANES), F32), pltpu.VMEM((1, LANES), F32)],
        compiler_params=_cparams(("arbitrary", "arbitrary")),
        name="route",
    )(logits)


def _row_copy(src_ref, src_row, dst_ref, dst_row, sem):
    shift, low = SUBLANES.bit_length() - 1, SUBLANES - 1
    return pltpu.make_async_copy(
        src_ref.at[src_row >> shift, :, pl.ds(src_row & low, 1), :],
        dst_ref.at[dst_row >> shift, :, pl.ds(dst_row & low, 1), :], sem)


def _for_rows(n, fn):
    def body(g, carry):
        for u in range(ROW_UNROLL):
            fn(g * ROW_UNROLL + u)
        return carry
    lax.fori_loop(0, n // ROW_UNROLL, body, 0)


def _dispatch_kernel(dest_ref, h4_ref, xs_ref, sem, *, tm):
    i = pl.program_id(0)

    def copies(r):
        tok = i * tm + r
        return [_row_copy(h4_ref, r, xs_ref, dest_ref[tok * TOP_K + k], sem) for k in range(TOP_K)]

    def issue(r):
        for cp in copies(r):
            cp.start()

    def drain(r):
        for cp in copies(r):
            cp.wait()

    _for_rows(tm, issue)
    _for_rows(tm, drain)


def _dispatch(dest_flat, h4, n_rows, d: Dims):
    t = h4.shape[0] * SUBLANES
    tm = d.tm_disp
    kern = functools.partial(_dispatch_kernel, tm=tm)
    return pl.pallas_call(
        kern,
        out_shape=jax.ShapeDtypeStruct((n_rows // SUBLANES,) + h4.shape[1:], F32),
        grid_spec=pltpu.PrefetchScalarGridSpec(
            num_scalar_prefetch=1,
            grid=(t // tm,),
            in_specs=[pl.BlockSpec((tm // SUBLANES,) + h4.shape[1:], lambda i, dest: (i, 0, 0, 0))],
            out_specs=pl.BlockSpec(memory_space=pl.ANY),
            scratch_shapes=[pltpu.SemaphoreType.DMA],
        ),
        compiler_params=_cparams(("arbitrary",)),
        name="dispatch",
    )(dest_flat, h4)


def _moe_kernel(te_ref, ts_ref, tv_ref, nu_ref,
                xs_ref, wg_ref, wu_ref, bg_ref, bu_ref, wd_ref, bd_ref,
                y_ref, xb_scr, *, tm):
    i = pl.program_id(0)
    j = pl.program_id(1)
    nc = xs_ref.shape[1]

    @pl.when(i < nu_ref[0])
    def _():
        @pl.when(j == 0)
        def _():
            valid = _row_tiles(lax.broadcasted_iota(jnp.int32, (tm, LANES), 0)) < tv_ref[i]
            for c in range(nc):
                xb_scr[:, c * LANES:(c + 1) * LANES] = (
                    jnp.where(valid, xs_ref[:, c], 0.0).reshape(tm, LANES).astype(BF16))
                y_ref[:, c] = jnp.broadcast_to(bd_ref[0, :, c * LANES:(c + 1) * LANES],
                                               (tm // SUBLANES, SUBLANES, LANES))

        def ffn(rows):
            tg = rows // SUBLANES
            x = xb_scr[:rows, :]
            gate = jnp.dot(x, wg_ref[0, 0], preferred_element_type=F32) + bg_ref[0]
            up = jnp.dot(x, wu_ref[0, 0], preferred_element_type=F32) + bu_ref[0]
            gate = jnp.minimum(gate, SWIGLU_LIMIT)
            up = jnp.clip(up, -SWIGLU_LIMIT, SWIGLU_LIMIT)
            act = ((up + 1.0) * gate * _sigmoid(SWIGLU_ALPHA * gate)).astype(BF16)
            for c in range(0, nc, 2):
                part = jnp.dot(act, wd_ref[0, :, c * LANES:(c + 2) * LANES].astype(BF16),
                               preferred_element_type=F32)
                y_ref[:tg, c] = y_ref[:tg, c] + _row_tiles(part[:, :LANES])
                y_ref[:tg, c + 1] = y_ref[:tg, c + 1] + _row_tiles(part[:, LANES:])

        @pl.when(tv_ref[i] > tm // 2)
        def _():
            ffn(tm)

        @pl.when(tv_ref[i] <= tm // 2)
        def _():
            ffn(tm // 2)


def _moe(tile_expert, tile_src, tile_valid, n_used, xs4, wgu, bgu, wd, bd, d: Dims):
    n_rows = xs4.shape[0] * SUBLANES
    tm, tf = d.tm_moe, d.tf_moe
    dm, ff = d.d_model, d.d_ff
    nj = ff // tf
    n_tiles = n_rows // tm
    blk4 = (tm // SUBLANES,) + xs4.shape[1:]

    def jj(i, j, nu):
        return jnp.where(i < nu[0], j, nj - 1)

    kern = functools.partial(_moe_kernel, tm=tm)
    return pl.pallas_call(
        kern,
        out_shape=jax.ShapeDtypeStruct(xs4.shape, F32),
        grid_spec=pltpu.PrefetchScalarGridSpec(
            num_scalar_prefetch=4,
            grid=(n_tiles, nj),
            in_specs=[
                pl.BlockSpec(blk4, lambda i, j, te, ts, tv, nu: (ts[i], 0, 0, 0)),
                pl.BlockSpec((1, 1, dm, tf), lambda i, j, te, ts, tv, nu: (te[i], jj(i, j, nu), 0, 0)),
                pl.BlockSpec((1, 1, dm, tf), lambda i, j, te, ts, tv, nu: (te[i], nj + jj(i, j, nu), 0, 0)),
                pl.BlockSpec((1, 1, tf), lambda i, j, te, ts, tv, nu: (te[i], 0, jj(i, j, nu))),
                pl.BlockSpec((1, 1, tf), lambda i, j, te, ts, tv, nu: (te[i], 0, nj + jj(i, j, nu))),
                pl.BlockSpec((1, tf, dm), lambda i, j, te, ts, tv, nu: (te[i], jj(i, j, nu), 0)),
                pl.BlockSpec((1, 1, dm), lambda i, j, te, ts, tv, nu: (te[i], 0, 0)),
            ],
            out_specs=pl.BlockSpec(blk4, lambda i, j, te, ts, tv, nu: (ts[i], 0, 0, 0)),
            scratch_shapes=[pltpu.VMEM((tm, dm), BF16)],
        ),
        compiler_params=_cparams(("arbitrary", "arbitrary")),
        name="moe",
    )(tile_expert, tile_src, tile_valid, n_used, xs4, wgu, wgu, bgu, bgu, wd, bd)


def _combine_kernel(dest_ref, y4_ref, gate_ref, x1_ref, o_ref, buf_a, buf_b, sem_a, sem_b, *, tm):
    i = pl.program_id(0)
    n = pl.num_programs(0)

    def copies(half_blk, r, buf, sem):
        tok = half_blk * tm + r
        return [_row_copy(y4_ref, dest_ref[tok * TOP_K + k], buf, k * tm + r, sem) for k in range(TOP_K)]

    def issue(half_blk, buf, sem):
        def row(r):
            for cp in copies(half_blk, r, buf, sem):
                cp.start()
        _for_rows(tm, row)

    def drain(half_blk, buf, sem):
        def row(r):
            for cp in copies(half_blk, r, buf, sem):
                cp.wait()
        _for_rows(tm, row)

    def reduce(buf, row0):
        gates = gate_ref[row0:row0 + tm, :]
        tg = tm // SUBLANES
        for c in range(buf.shape[1]):
            acc = x1_ref[row0:row0 + tm, c * LANES:(c + 1) * LANES]
            for k in range(TOP_K):
                acc = acc + gates[:, k:k + 1] * buf[k * tg:(k + 1) * tg, c].reshape(tm, LANES)
            o_ref[row0:row0 + tm, c * LANES:(c + 1) * LANES] = acc

    @pl.when(i == 0)
    def _():
        issue(0, buf_a, sem_a)

    issue(2 * i + 1, buf_b, sem_b)
    drain(2 * i, buf_a, sem_a)
    reduce(buf_a, 0)

    @pl.when(i < n - 1)
    def _():
        issue(2 * i + 2, buf_a, sem_a)

    drain(2 * i + 1, buf_b, sem_b)
    reduce(buf_b, tm)


def _combine(dest_flat, y4, gates, x1, d: Dims):
    t, dm = x1.shape
    tm = d.tm_comb
    kern = functools.partial(_combine_kernel, tm=tm)
    buf = pltpu.VMEM((TOP_K * tm // SUBLANES,) + y4.shape[1:], F32)
    return pl.pallas_call(
        kern,
        out_shape=jax.ShapeDtypeStruct((t, dm), F32),
        grid_spec=pltpu.PrefetchScalarGridSpec(
            num_scalar_prefetch=1,
            grid=(t // (2 * tm),),
            in_specs=[
                pl.BlockSpec(memory_space=pl.ANY),
                pl.BlockSpec((2 * tm, TOP_K), lambda i, dest: (i, 0)),
                pl.BlockSpec((2 * tm, dm), lambda i, dest: (i, 0)),
            ],
            out_specs=pl.BlockSpec((2 * tm, dm), lambda i, dest: (i, 0)),
            scratch_shapes=[buf, buf, pltpu.SemaphoreType.DMA, pltpu.SemaphoreType.DMA],
        ),
        compiler_params=_cparams(("arbitrary",)),
        name="combine",
    )(dest_flat, y4, gates, x1)


def _pad_lanes(v, fill=0.0):
    v = v.reshape(1, -1).astype(F32)
    return jnp.pad(v, ((0, 0), (0, LANES - v.shape[1])), constant_values=fill)


def _layer(d: Dims, x, positions, g_mix, w_in, g_q_latent, w_uq, g_kv_latent, w_ukv,
           g_q_nope, g_q_rope, g_k_nope, g_k_rope, w_pool, s_pool, w_out,
           g_ffn, w_router, b_router, w_gate_up, b_gate_up, w_down, b_down):
    b, s, dm, h = d.batch, d.seq, d.d_model, d.n_heads
    t = b * s
    assert d.nope == LANES and d.v_dim == LANES and d.rope <= LANES and dm % LANES == 0
    assert d.q_rank == d.kv_rank and d.pool_width % d.q_rank == 0 and dm % d.pool_width == 0
    assert d.tq % d.tk == 0 and s % d.tq == 0

    o_q, o_kv = 0, d.q_rank
    o_kr = o_kv + d.kv_rank
    o_pool = o_kr + d.rope
    o_ga = o_pool + d.pool_width
    o_gp = o_ga + dm
    n_pool = -(-(d.q_rank + d.kv_rank) // d.pool_width) * d.pool_width
    n_ga = -(-(n_pool + d.pool_width) // dm) * dm
    n_gp = n_ga + dm
    n_kr = n_gp + dm
    nz = -(-(n_kr + LANES) // d.tn_in) * d.tn_in
    zeros = lambda n: jnp.zeros((dm, n), w_in.dtype)
    w_in_r = jnp.concatenate([
        w_in[:, o_q:o_kr], zeros(n_pool - (d.q_rank + d.kv_rank)),
        w_in[:, o_pool:o_ga], zeros(n_ga - n_pool - d.pool_width),
        w_in[:, o_ga:o_gp], w_in[:, o_gp:],
        w_in[:, o_kr:o_pool], zeros(nz - n_kr - d.rope)], axis=1).astype(BF16)

    qk = d.nope + d.rope
    half = d.rope // 2
    wq3 = w_uq.reshape(d.q_rank, h, qk)
    wq = jnp.concatenate([wq3, jnp.zeros((d.q_rank, h, 2 * LANES - qk), w_uq.dtype)], axis=2)
    wqt = wq.reshape(d.q_rank, h * 2 * LANES).T.astype(BF16)
    wkv3 = w_ukv.reshape(d.kv_rank, h, d.nope + d.v_dim)
    wk = wkv3[:, :, :d.nope].reshape(d.kv_rank, h * d.nope).astype(BF16)
    wvt = wkv3[:, :, d.nope:].reshape(d.kv_rank, h * d.v_dim).T.astype(BF16)

    inv_freq = ROPE_THETA ** (-np.arange(0, d.rope, 2, dtype=np.float32) / d.rope)
    invf = np.zeros((1, LANES), np.float32)
    invf[0, :half] = inv_freq
    invf[0, half:d.rope] = inv_freq
    sgn = np.zeros((1, LANES), np.float32)
    sgn[0, :half] = -1.0
    sgn[0, half:d.rope] = 1.0
    tmq = d.tk
    invft = np.ascontiguousarray(np.broadcast_to(inv_freq[:, None], (half, tmq)))
    consts = (jnp.asarray(invf), jnp.asarray(sgn), jnp.asarray(invft))
    bcast = lambda v: jnp.broadcast_to(v.astype(F32)[:, None], (v.shape[0], tmq))

    x2d = x.reshape(t, dm)
    pos_f = positions.reshape(t).astype(F32)

    z = _inproj(x2d, g_mix.reshape(1, dm), w_in_r, d)
    qt, k, vt = _qkv(z, pos_f.reshape(t, 1), pos_f.reshape(1, t), consts,
                     g_q_latent.reshape(1, -1), g_kv_latent.reshape(1, -1),
                     bcast(g_q_nope), bcast(g_q_rope[:half]), bcast(g_q_rope[half:]),
                     _pad_lanes(g_k_nope), _pad_lanes(g_k_rope), wqt, wk, wvt, d, n_kr)
    ya, wgu_bf = _attn(qt, k, vt, w_gate_up.reshape(d.n_experts * dm, 2 * d.d_ff), d)
    wr = jnp.pad(w_router.astype(F32), ((0, 0), (0, LANES - d.n_experts)))
    wr_hi = wr.astype(BF16)
    wr = jnp.concatenate([wr_hi, (wr - wr_hi.astype(F32)).astype(BF16)], axis=1)
    x1, h4, logits = _mixout(z, ya, x2d, w_pool.astype(BF16), s_pool.reshape(1, dm), w_out.astype(BF16),
                             g_ffn.reshape(1, dm), wr, _pad_lanes(b_router), d, n_pool, n_ga, n_gp)

    dest, gates, counts = _route(logits, d)
    tm = d.tm_moe
    n_rows = t * TOP_K + d.n_experts * tm
    n_tiles = n_rows // tm
    cnt = counts[0, :d.n_experts].astype(jnp.int32)
    ends = jnp.cumsum((cnt + tm - 1) // tm * tm)
    n_used = ends[-1] // tm
    tile_ids = jnp.minimum(jnp.arange(n_tiles, dtype=jnp.int32), n_used - 1)
    tile_expert = jnp.sum(ends[None, :] <= (tile_ids * tm)[:, None], axis=1).astype(jnp.int32)
    tile_expert = jnp.minimum(tile_expert, d.n_experts - 1)
    starts = ends - (cnt + tm - 1) // tm * tm
    tile_valid = jnp.clip(starts[tile_expert] + cnt[tile_expert] - tile_ids * tm, 0, tm).astype(jnp.int32)

    dest_flat = dest.reshape(t * TOP_K)
    xs4 = _dispatch(dest_flat, h4, n_rows, d)
    y4 = _moe(tile_expert, tile_ids, tile_valid, n_used.reshape(1).astype(jnp.int32), xs4,
              wgu_bf, b_gate_up.reshape(d.n_experts, 1, 2 * d.d_ff),
              w_down, b_down.reshape(d.n_experts, 1, dm), d)
    out = _combine(dest_flat, y4, gates, x1, d)
    return out.reshape(b, s, dm)


def kernel(x, positions, g_mix, w_in, g_q_latent, w_uq, g_kv_latent, w_ukv, g_q_nope, g_q_rope, g_k_nope, g_k_rope, w_pool, s_pool, w_out, g_ffn, w_router, b_router, w_gate_up, b_gate_up, w_down, b_down):
    return _layer(Dims(), x, positions, g_mix, w_in, g_q_latent, w_uq, g_kv_latent, w_ukv,
                  g_q_nope, g_q_rope, g_k_nope, g_k_rope, w_pool, s_pool, w_out,
                  g_ffn, w_router, b_router, w_gate_up, b_gate_up, w_down, b_down)
```

```python
import functools
import math
from typing import NamedTuple

import numpy as np
import jax
import jax.numpy as jnp
from jax import lax
from jax.experimental import pallas as pl
from jax.experimental.pallas import tpu as pltpu

F32 = jnp.float32
BF16 = jnp.bfloat16

RMS_EPS = 1e-6
NEG_INF = -1e30
ROPE_THETA = 10000.0
SWIGLU_LIMIT = 7.0
SWIGLU_ALPHA = 1.702
POOL_WINDOWS = (2, 4, 8, 16)
TOP_K = 4

LANES = 128
SUBLANES = 8
POOL_HALO = 16
VMEM_LIMIT = 56 * 1024 * 1024


class Dims(NamedTuple):
    batch: int = 2
    seq: int = 8192
    d_model: int = 2048
    n_heads: int = 16
    nope: int = 128
    rope: int = 64
    v_dim: int = 128
    q_rank: int = 512
    kv_rank: int = 512
    pool_width: int = 1024
    n_experts: int = 32
    d_ff: int = 2048
    tm_in: int = 1024
    tn_in: int = 1664
    tq: int = 2048
    tk: int = 512
    attn_gw: int = 512
    tm_mix: int = 256
    tm_route: int = 512
    tm_disp: int = 256
    tm_moe: int = 512
    tf_moe: int = 512
    tm_comb: int = 128


def _cparams(sem, vmem=VMEM_LIMIT):
    return pltpu.CompilerParams(dimension_semantics=sem, vmem_limit_bytes=vmem)


def _const_spec(shape):
    nd = len(shape)
    return pl.BlockSpec(shape, lambda *_: (0,) * nd, pipeline_mode=pl.Buffered(1))


def _rms(t, gain):
    return t * lax.rsqrt(jnp.mean(t * t, axis=-1, keepdims=True) + RMS_EPS) * gain


def _sigmoid(t):
    return 0.5 * jnp.tanh(0.5 * t) + 0.5


def _row_tiles(v):
    return v.reshape(v.shape[0] // SUBLANES, SUBLANES, LANES)


def _inproj_kernel(x_ref, g_ref, w_ref, z_ref, h_scr):
    @pl.when(pl.program_id(1) == 0)
    def _():
        h_scr[...] = _rms(x_ref[...], g_ref[...]).astype(BF16)

    z_ref[...] = jnp.dot(h_scr[...], w_ref[...], preferred_element_type=F32).astype(z_ref.dtype)


def _inproj(x2d, g_mix, w_in_r, d: Dims):
    t, dm = x2d.shape
    nz = w_in_r.shape[1]
    tm, tn = d.tm_in, d.tn_in
    return pl.pallas_call(
        _inproj_kernel,
        out_shape=jax.ShapeDtypeStruct((t, nz), BF16),
        grid=(t // tm, nz // tn),
        in_specs=[
            pl.BlockSpec((tm, dm), lambda i, j: (i, 0)),
            pl.BlockSpec((1, dm), lambda i, j: (0, 0)),
            pl.BlockSpec((dm, tn), lambda i, j: (0, j)),
        ],
        out_specs=pl.BlockSpec((tm, tn), lambda i, j: (i, j)),
        scratch_shapes=[pltpu.VMEM((tm, dm), BF16)],
        compiler_params=_cparams(("parallel", "arbitrary")),
        name="inproj",
    )(x2d, g_mix, w_in_r)


def _rope_tile(t, cosf, sinf, half):
    lane = lax.broadcasted_iota(jnp.int32, t.shape, 1)
    swapped = jnp.where(lane < half, pltpu.roll(t, LANES - half, 1), pltpu.roll(t, half, 1))
    return t * cosf + swapped * sinf


def _qkv_kernel(zq_ref, zkv_ref, zkr_ref, pos_ref, posr_ref, invf_ref, sgn_ref, invft_ref,
                gql_ref, gkvl_ref, gqnt_ref, gqr1_ref, gqr2_ref, gkn_ref, gkr_ref,
                wqt_ref, wk_ref, wvt_ref,
                qt_ref, k_ref, vt_ref, *, n_heads, rope, qscale):
    half = rope // 2
    hw = 2 * LANES
    ang = pos_ref[...] * invf_ref[...]
    cosf = jnp.cos(ang)
    sinf = jnp.sin(ang) * sgn_ref[...]
    angt = invft_ref[...] * posr_ref[...]
    cost = jnp.cos(angt)
    sint = jnp.sin(angt)

    hq = _rms(zq_ref[...].astype(F32), gql_ref[...])
    hkv = _rms(zkv_ref[...].astype(F32), gkvl_ref[...])
    hqt = hq.T.astype(BF16)
    hkvt = hkv.T.astype(BF16)
    hkv = hkv.astype(BF16)

    kr = zkr_ref[...].astype(F32)
    kr = kr * lax.rsqrt(jnp.sum(kr * kr, axis=-1, keepdims=True) / rope + RMS_EPS) * gkr_ref[...]
    kpe = _rope_tile(kr, cosf, sinf, half).astype(BF16)

    for h in range(n_heads):
        qh = jnp.dot(wqt_ref[h * hw:(h + 1) * hw, :], hqt, preferred_element_type=F32)
        qn = qh[:LANES]
        qn = qn * lax.rsqrt(jnp.mean(qn * qn, axis=0, keepdims=True) + RMS_EPS) * gqnt_ref[...]
        t1 = qh[LANES:LANES + half]
        t2 = qh[LANES + half:LANES + rope]
        ss = jnp.sum(t1 * t1, axis=0, keepdims=True) + jnp.sum(t2 * t2, axis=0, keepdims=True)
        r = lax.rsqrt(ss / rope + RMS_EPS)
        t1 = t1 * r * gqr1_ref[...]
        t2 = t2 * r * gqr2_ref[...]
        qt_ref[h * hw:h * hw + LANES, :] = (qn * qscale).astype(BF16)
        qt_ref[h * hw + LANES:h * hw + LANES + half, :] = ((t1 * cost - t2 * sint) * qscale).astype(BF16)
        qt_ref[h * hw + LANES + half:h * hw + LANES + rope, :] = ((t2 * cost + t1 * sint) * qscale).astype(BF16)
        qt_ref[h * hw + LANES + rope:(h + 1) * hw, :] = jnp.zeros((LANES - rope, qt_ref.shape[1]), BF16)

        kh = jnp.dot(hkv, wk_ref[:, h * LANES:(h + 1) * LANES], preferred_element_type=F32)
        k_ref[:, h * hw:h * hw + LANES] = _rms(kh, gkn_ref[...]).astype(BF16)
        k_ref[:, h * hw + LANES:(h + 1) * hw] = kpe

    for c in range(wvt_ref.shape[0] // hw):
        vt = jnp.dot(wvt_ref[c * hw:(c + 1) * hw, :], hkvt, preferred_element_type=F32)
        vt_ref[0, c * hw:(c + 1) * hw, :] = vt.astype(BF16)


def _qkv(z, pos_col, pos_row, consts, gql, gkvl, gqnt, gqr1, gqr2, gkn, gkr, wqt, wk, wvt, d: Dims, off_kr):
    t = z.shape[0]
    tm = d.tk
    invf, sgn, invft = consts
    hq = d.n_heads * 2 * LANES
    hv = d.n_heads * d.v_dim
    row = lambda a: pl.BlockSpec(a.shape, lambda i: (0, 0))
    kern = functools.partial(_qkv_kernel, n_heads=d.n_heads, rope=d.rope,
                             qscale=math.log2(math.e) / math.sqrt(d.nope + d.rope))
    return pl.pallas_call(
        kern,
        out_shape=(jax.ShapeDtypeStruct((hq, t), BF16),
                   jax.ShapeDtypeStruct((t, hq), BF16),
                   jax.ShapeDtypeStruct((t // tm, hv, tm), BF16)),
        grid=(t // tm,),
        in_specs=[
            pl.BlockSpec((tm, d.q_rank), lambda i: (i, 0)),
            pl.BlockSpec((tm, d.kv_rank), lambda i: (i, d.q_rank // d.kv_rank)),
            pl.BlockSpec((tm, LANES), lambda i: (i, off_kr // LANES)),
            pl.BlockSpec((tm, 1), lambda i: (i, 0)),
            pl.BlockSpec((1, tm), lambda i: (0, i)),
            row(invf), row(sgn), row(invft),
            row(gql), row(gkvl), row(gqnt), row(gqr1), row(gqr2), row(gkn), row(gkr),
            _const_spec(wqt.shape), _const_spec(wk.shape), _const_spec(wvt.shape),
        ],
        out_specs=(pl.BlockSpec((hq, tm), lambda i: (0, i)),
                   pl.BlockSpec((tm, hq), lambda i: (i, 0)),
                   pl.BlockSpec((1, hv, tm), lambda i: (i, 0, 0))),
        compiler_params=_cparams(("parallel",)),
        name="qkv",
    )(z, z, z, pos_col, pos_row, invf, sgn, invft, gql, gkvl, gqnt, gqr1, gqr2, gkn, gkr, wqt, wk, wvt)


def _attn_kernel(qt_ref, k_ref, vt_ref, wgu_ref, o_ref, wgu_bf_ref, *scr, tq, tk, gw):
    i = pl.program_id(2)
    n_sub = tq // tk
    ng = tq // gw
    cw = wgu_bf_ref.shape[3]
    for c in range(wgu_bf_ref.shape[1]):
        wgu_bf_ref[0, c] = wgu_ref[:, c * cw:(c + 1) * cw].astype(BF16)
    m_scr, l_scr, acc_scr = scr[:ng], scr[ng:2 * ng], scr[2 * ng:3 * ng]
    s_scr, cm_scr = scr[3 * ng:4 * ng], scr[4 * ng:]
    for g in range(ng):
        m_scr[g][...] = jnp.full(m_scr[g].shape, NEG_INF, F32)
        l_scr[g][...] = jnp.zeros(l_scr[g].shape, F32)
        acc_scr[g][...] = jnp.zeros(acc_scr[g].shape, F32)

    def scores(c, g):
        kc = k_ref[pl.ds(pl.multiple_of(c * tk, tk), tk), :]
        return jnp.dot(kc, qt_ref[:, g * gw:(g + 1) * gw], preferred_element_type=F32)

    def softmax_pv(c, g, s, cmax):
        m_old = m_scr[g][...]
        m_new = jnp.maximum(m_old, cmax)
        p = jnp.exp2(s - m_new)
        alpha = jnp.exp2(m_old - m_new)
        l_scr[g][...] = alpha * l_scr[g][...] + jnp.sum(p, axis=0, keepdims=True)
        pv = jnp.dot(vt_ref[c], p.astype(BF16), preferred_element_type=F32)
        acc_scr[g][...] = alpha * acc_scr[g][...] + pv
        m_scr[g][...] = m_new

    def stash(g, s):
        s_scr[g][...] = s
        cm_scr[g][...] = jnp.max(s, axis=0, keepdims=True)

    n_full = i * n_sub

    @pl.when(n_full > 0)
    def _():
        for g in range(ng):
            stash(g, scores(0, g))

        def body(c, carry):
            nxt = [scores(c + 1, g) for g in range(ng)]
            for g in range(ng):
                softmax_pv(c, g, s_scr[g][...], cm_scr[g][...])
            for g in range(ng):
                stash(g, nxt[g])
            return carry

        lax.fori_loop(0, n_full - 1, body, 0)
        for g in range(ng):
            softmax_pv(n_full - 1, g, s_scr[g][...], cm_scr[g][...])

    key = lax.broadcasted_iota(jnp.int32, (tk, gw), 0)
    qry = lax.broadcasted_iota(jnp.int32, (tk, gw), 1)
    steps = []
    for cc in range(n_sub):
        for g in range(ng):
            k_lo, k_hi = cc * tk, (cc + 1) * tk - 1
            q_lo, q_hi = g * gw, (g + 1) * gw - 1
            if k_lo > q_hi:
                continue
            steps.append((cc, g, k_hi > q_lo, k_lo, q_lo))

    def diag_scores(step):
        cc, g, masked, k_lo, q_lo = step
        s = scores(n_full + cc, g)
        if masked:
            s = jnp.where((key + k_lo) <= (qry + q_lo), s, NEG_INF)
        return s

    s_cur = diag_scores(steps[0])
    for j, step in enumerate(steps):
        s_nxt = diag_scores(steps[j + 1]) if j + 1 < len(steps) else None
        softmax_pv(n_full + step[0], step[1], s_cur, jnp.max(s_cur, axis=0, keepdims=True))
        s_cur = s_nxt

    for g in range(ng):
        o = acc_scr[g][...] / l_scr[g][...]
        o_ref[g * gw:(g + 1) * gw, :] = o.T.astype(o_ref.dtype)


def _attn(qt, k, vt, wgu2d, d: Dims):
    b, s, h = d.batch, d.seq, d.n_heads
    tq, tk = d.tq, d.tk
    gw = min(d.attn_gw, tq)
    ng = tq // gw
    nq = s // tq
    n_steps = b * h * nq
    ru = wgu2d.shape[0] // n_steps
    dm, tf = d.d_model, d.tf_moe
    spe = dm // ru
    ncol = wgu2d.shape[1] // tf
    assert ru * n_steps == wgu2d.shape[0] and spe * ru == dm
    step = lambda bi, hi, i: ((bi * h + hi) * nq + i, 0)
    kern = functools.partial(_attn_kernel, tq=tq, tk=tk, gw=gw)
    return pl.pallas_call(
        kern,
        out_shape=(jax.ShapeDtypeStruct((b * s, h * d.v_dim), BF16),
                   jax.ShapeDtypeStruct((d.n_experts, ncol, dm, tf), BF16)),
        grid=(b, h, nq),
        in_specs=[
            pl.BlockSpec((2 * LANES, tq), lambda bi, hi, i: (hi, bi * nq + i)),
            pl.BlockSpec((s, 2 * LANES), lambda bi, hi, i: (bi, hi)),
            pl.BlockSpec((s // tk, d.v_dim, tk), lambda bi, hi, i: (bi, hi, 0)),
            pl.BlockSpec((ru, wgu2d.shape[1]), step),
        ],
        out_specs=(pl.BlockSpec((tq, d.v_dim), lambda bi, hi, i: (bi * nq + i, hi)),
                   pl.BlockSpec((1, ncol, ru, tf),
                                lambda bi, hi, i: (step(bi, hi, i)[0] // spe, 0, step(bi, hi, i)[0] % spe, 0))),
        scratch_shapes=([pltpu.VMEM((1, gw), F32)] * (2 * ng) + [pltpu.VMEM((d.v_dim, gw), F32)] * ng
                        + [pltpu.VMEM((tk, gw), F32)] * ng + [pltpu.VMEM((1, gw), F32)] * ng),
        compiler_params=_cparams(("parallel", "parallel", "arbitrary")),
        name="attn",
    )(qt, k, vt, wgu2d)


def _mixout_kernel(u_ref, halo_ref, ga_ref, gp_ref, ya_ref, x_ref,
                   wpool_ref, spool_ref, wout_ref, gffn_ref, wr_ref, br_ref,
                   x1_ref, h4_ref, lg_ref, merged_scr, *, tm, seq, n_groups):
    i = pl.program_id(0)
    row0 = (i * tm) % seq
    gdim = u_ref.shape[1] // n_groups
    odim = ga_ref.shape[1] // n_groups
    pos = row0 + lax.broadcasted_iota(jnp.int32, (tm, 1), 0)
    halo_on = (row0 > 0).astype(F32)

    for g in range(n_groups):
        w = POOL_WINDOWS[g]
        ug = u_ref[:, g * gdim:(g + 1) * gdim].astype(F32)
        hg = halo_ref[:, g * gdim:(g + 1) * gdim].astype(F32) * halo_on
        ext = jnp.concatenate([hg, ug], axis=0)
        shift = 1
        while shift < w:
            ext = ext + pltpu.roll(ext, shift, 0)
            shift *= 2
        wsum = ext[POOL_HALO:, :]
        count = jnp.minimum(pos + 1, w).astype(F32)
        pooled = (wsum / count - ug).astype(BF16)
        yp = jnp.dot(pooled, wpool_ref[g], preferred_element_type=F32)
        yp = yp * spool_ref[:, g * odim:(g + 1) * odim]
        sl = slice(g * odim, (g + 1) * odim)
        merged = (_sigmoid(ga_ref[:, sl].astype(F32)) * ya_ref[:, sl].astype(F32)
                  + _sigmoid(gp_ref[:, sl].astype(F32)) * yp)
        merged_scr[:, sl] = merged.astype(BF16)

    x1 = x_ref[...] + jnp.dot(merged_scr[...], wout_ref[...], preferred_element_type=F32)
    x1_ref[...] = x1
    hn = _rms(x1, gffn_ref[...])
    hi = hn.astype(BF16)
    lo = (hn - hi.astype(F32)).astype(BF16)
    r = jnp.dot(jnp.concatenate([hi, lo], axis=0), wr_ref[...], preferred_element_type=F32)
    lg_ref[...] = (r[:tm, :LANES] + r[:tm, LANES:]) + (r[tm:, :LANES] + r[tm:, LANES:]) + br_ref[...]
    for c in range(h4_ref.shape[1]):
        h4_ref[:, c] = _row_tiles(hn[:, c * LANES:(c + 1) * LANES])


def _mixout(z, ya, x2d, wpool, spool, wout, gffn, wr, br, d: Dims, off_pool, off_ga, off_gp):
    t, dm = x2d.shape
    tm = d.tm_mix
    pw = d.pool_width
    n_groups = len(POOL_WINDOWS)
    hb = tm // POOL_HALO
    nc = dm // LANES
    kern = functools.partial(_mixout_kernel, tm=tm, seq=d.seq, n_groups=n_groups)
    return pl.pallas_call(
        kern,
        out_shape=(jax.ShapeDtypeStruct((t, dm), F32),
                   jax.ShapeDtypeStruct((t // SUBLANES, nc, SUBLANES, LANES), F32),
                   jax.ShapeDtypeStruct((t, LANES), F32)),
        grid=(t // tm,),
        in_specs=[
            pl.BlockSpec((tm, pw), lambda i: (i, off_pool // pw)),
            pl.BlockSpec((POOL_HALO, pw), lambda i: (jnp.maximum(i * hb - 1, 0), off_pool // pw)),
            pl.BlockSpec((tm, dm), lambda i: (i, off_ga // dm)),
            pl.BlockSpec((tm, dm), lambda i: (i, off_gp // dm)),
            pl.BlockSpec((tm, dm), lambda i: (i, 0)),
            pl.BlockSpec((tm, dm), lambda i: (i, 0)),
            _const_spec(wpool.shape), _const_spec(spool.shape), _const_spec(wout.shape),
            _const_spec(gffn.shape), _const_spec(wr.shape), _const_spec(br.shape),
        ],
        out_specs=(pl.BlockSpec((tm, dm), lambda i: (i, 0)),
                   pl.BlockSpec((tm // SUBLANES, nc, SUBLANES, LANES), lambda i: (i, 0, 0, 0)),
                   pl.BlockSpec((tm, LANES), lambda i: (i, 0))),
        scratch_shapes=[pltpu.VMEM((tm, dm), BF16)],
        compiler_params=_cparams(("parallel",)),
        name="mixout",
    )(z, z, z, z, ya, x2d, wpool, spool, wout, gffn, wr, br)


def _lane_excl_cumsum(v):
    lane = lax.broadcasted_iota(jnp.int32, v.shape, 1)
    inc = v
    shift = 1
    while shift < LANES:
        inc = inc + jnp.where(lane >= shift, pltpu.roll(inc, shift, 1), 0.0)
        shift *= 2
    return inc - v


def _route_kernel(lg_ref, dest_ref, gate_ref, cnt_ref, carry_scr, base_scr, *, tm, n_experts, tile):
    phase = pl.program_id(0)
    i = pl.program_id(1)
    lane = lax.broadcasted_iota(jnp.int32, (tm, LANES), 1)
    work = jnp.where(lane < n_experts, lg_ref[...], -jnp.inf)

    vals, hots = [], []
    for _ in range(TOP_K):
        m = jnp.max(work, axis=-1, keepdims=True)
        first = jnp.min(jnp.where(work == m, lane, LANES), axis=-1, keepdims=True)
        hot = lane == first
        vals.append(m)
        hots.append(hot)
        work = jnp.where(hot, -jnp.inf, work)
    chosen = jnp.zeros((tm, LANES), F32)
    for hot in hots:
        chosen = chosen + hot.astype(F32)

    @pl.when((phase == 0) & (i == 0))
    def _():
        carry_scr[...] = jnp.zeros(carry_scr.shape, F32)

    @pl.when((phase == 1) & (i == 0))
    def _():
        counts = carry_scr[...]
        padded = jnp.ceil(counts / tile) * tile
        base_scr[...] = _lane_excl_cumsum(padded)
        cnt_ref[...] = counts
        carry_scr[...] = jnp.zeros(carry_scr.shape, F32)

    @pl.when(phase == 1)
    def _():
        r = lax.broadcasted_iota(jnp.int32, (tm, tm), 0)
        c = lax.broadcasted_iota(jnp.int32, (tm, tm), 1)
        tri = (c < r).astype(BF16)
        before = jnp.dot(tri, chosen.astype(BF16), preferred_element_type=F32)
        slot = before + carry_scr[...] + base_scr[...]
        exps = [jnp.exp(v - vals[0]) for v in vals]
        den = exps[0] + exps[1] + exps[2] + exps[3]
        for k in range(TOP_K):
            dk = jnp.sum(jnp.where(hots[k], slot, 0.0), axis=-1, keepdims=True)
            dest_ref[:, k:k + 1] = dk.astype(jnp.int32)
            gate_ref[:, k:k + 1] = exps[k] / den

    carry_scr[...] = carry_scr[...] + jnp.sum(chosen, axis=0, keepdims=True)


def _route(logits, d: Dims):
    t = logits.shape[0]
    tm = d.tm_route
    kern = functools.partial(_route_kernel, tm=tm, n_experts=d.n_experts, tile=d.tm_moe)
    return pl.pallas_call(
        kern,
        out_shape=(jax.ShapeDtypeStruct((t, TOP_K), jnp.int32),
                   jax.ShapeDtypeStruct((t, TOP_K), F32),
                   jax.ShapeDtypeStruct((1, LANES), F32)),
        grid=(2, t // tm),
        in_specs=[pl.BlockSpec((tm, LANES), lambda p, i: (i, 0))],
        out_specs=(pl.BlockSpec((tm, TOP_K), lambda p, i: (i * p, 0)),
                   pl.BlockSpec((tm, TOP_K), lambda p, i: (i * p, 0)),
                   pl.BlockSpec((1, LANES), lambda p, i: (0, 0))),
        scratch_shapes=[pltpu.VMEM((1, LANES), F32), pltpu.VMEM((1, LANES), F32)],
        compiler_params=_cparams(("arbitrary", "arbitrary")),
        name="route",
    )(logits)


def _row_ref(ref, group, sub):
    return ref.at[group, :, pl.ds(sub, 1), :]


def _split_row(row):
    return row >> (SUBLANES.bit_length() - 1), row & (SUBLANES - 1)


def _for_rows(n, fn):
    def body(g, carry):
        for u in range(SUBLANES):
            fn(g, u)
        return carry
    lax.fori_loop(0, n // SUBLANES, body, 0)


def _dispatch_kernel(dest_ref, h4_ref, xs_ref, sem, *, tm):
    i = pl.program_id(0)

    def copies(g, u):
        tok = i * tm + g * SUBLANES + u
        src = _row_ref(h4_ref, g, u)
        return [pltpu.make_async_copy(src, _row_ref(xs_ref, *_split_row(dest_ref[tok * TOP_K + k])), sem)
                for k in range(TOP_K)]

    def issue(g, u):
        for cp in copies(g, u):
            cp.start()

    def drain(g, u):
        for cp in copies(g, u):
            cp.wait()

    _for_rows(tm, issue)
    _for_rows(tm, drain)


def _dispatch(dest_flat, h4, n_rows, d: Dims):
    t = h4.shape[0] * SUBLANES
    tm = d.tm_disp
    kern = functools.partial(_dispatch_kernel, tm=tm)
    return pl.pallas_call(
        kern,
        out_shape=jax.ShapeDtypeStruct((n_rows // SUBLANES,) + h4.shape[1:], F32),
        grid_spec=pltpu.PrefetchScalarGridSpec(
            num_scalar_prefetch=1,
            grid=(t // tm,),
            in_specs=[pl.BlockSpec((tm // SUBLANES,) + h4.shape[1:], lambda i, dest: (i, 0, 0, 0))],
            out_specs=pl.BlockSpec(memory_space=pl.ANY),
            scratch_shapes=[pltpu.SemaphoreType.DMA],
        ),
        compiler_params=_cparams(("arbitrary",)),
        name="dispatch",
    )(dest_flat, h4)


def _moe_kernel(te_ref, ts_ref, tv_ref, nu_ref,
                xs_ref, wg_ref, wu_ref, bg_ref, bu_ref, wd_ref, bd_ref,
                y_ref, xb_scr, *, tm):
    i = pl.program_id(0)
    j = pl.program_id(1)
    nc = xs_ref.shape[1]

    @pl.when(i < nu_ref[0])
    def _():
        @pl.when(j == 0)
        def _():
            valid = _row_tiles(lax.broadcasted_iota(jnp.int32, (tm, LANES), 0)) < tv_ref[i]
            for c in range(nc):
                xb_scr[:, c * LANES:(c + 1) * LANES] = (
                    jnp.where(valid, xs_ref[:, c], 0.0).reshape(tm, LANES).astype(BF16))
                y_ref[:, c] = jnp.broadcast_to(bd_ref[0, :, c * LANES:(c + 1) * LANES],
                                               (tm // SUBLANES, SUBLANES, LANES))

        def ffn(rows):
            tg = rows // SUBLANES
            x = xb_scr[:rows, :]
            gate = jnp.dot(x, wg_ref[0, 0], preferred_element_type=F32) + bg_ref[0]
            up = jnp.dot(x, wu_ref[0, 0], preferred_element_type=F32) + bu_ref[0]
            gate = jnp.minimum(gate, SWIGLU_LIMIT)
            up = jnp.clip(up, -SWIGLU_LIMIT, SWIGLU_LIMIT)
            act = ((up + 1.0) * gate * _sigmoid(SWIGLU_ALPHA * gate)).astype(BF16)
            for c in range(0, nc, 2):
                part = jnp.dot(act, wd_ref[0, :, c * LANES:(c + 2) * LANES].astype(BF16),
                               preferred_element_type=F32)
                y_ref[:tg, c] = y_ref[:tg, c] + _row_tiles(part[:, :LANES])
                y_ref[:tg, c + 1] = y_ref[:tg, c + 1] + _row_tiles(part[:, LANES:])

        @pl.when(tv_ref[i] > tm // 2)
        def _():
            ffn(tm)

        @pl.when(tv_ref[i] <= tm // 2)
        def _():
            ffn(tm // 2)


def _moe(tile_expert, tile_src, tile_valid, n_used, xs4, wgu, bgu, wd, bd, d: Dims):
    n_rows = xs4.shape[0] * SUBLANES
    tm, tf = d.tm_moe, d.tf_moe
    dm, ff = d.d_model, d.d_ff
    nj = ff // tf
    n_tiles = n_rows // tm
    blk4 = (tm // SUBLANES,) + xs4.shape[1:]

    def jj(i, j, nu):
        return jnp.where(i < nu[0], j, nj - 1)

    kern = functools.partial(_moe_kernel, tm=tm)
    return pl.pallas_call(
        kern,
        out_shape=jax.ShapeDtypeStruct(xs4.shape, F32),
        grid_spec=pltpu.PrefetchScalarGridSpec(
            num_scalar_prefetch=4,
            grid=(n_tiles, nj),
            in_specs=[
                pl.BlockSpec(blk4, lambda i, j, te, ts, tv, nu: (ts[i], 0, 0, 0)),
                pl.BlockSpec((1, 1, dm, tf), lambda i, j, te, ts, tv, nu: (te[i], jj(i, j, nu), 0, 0)),
                pl.BlockSpec((1, 1, dm, tf), lambda i, j, te, ts, tv, nu: (te[i], nj + jj(i, j, nu), 0, 0)),
                pl.BlockSpec((1, 1, tf), lambda i, j, te, ts, tv, nu: (te[i], 0, jj(i, j, nu))),
                pl.BlockSpec((1, 1, tf), lambda i, j, te, ts, tv, nu: (te[i], 0, nj + jj(i, j, nu))),
                pl.BlockSpec((1, tf, dm), lambda i, j, te, ts, tv, nu: (te[i], jj(i, j, nu), 0)),
                pl.BlockSpec((1, 1, dm), lambda i, j, te, ts, tv, nu: (te[i], 0, 0)),
            ],
            out_specs=pl.BlockSpec(blk4, lambda i, j, te, ts, tv, nu: (ts[i], 0, 0, 0)),
            scratch_shapes=[pltpu.VMEM((tm, dm), BF16)],
        ),
        compiler_params=_cparams(("arbitrary", "arbitrary")),
        name="moe",
    )(tile_expert, tile_src, tile_valid, n_used, xs4, wgu, wgu, bgu, bgu, wd, bd)


def _combine_kernel(dest_ref, y4_ref, gate_ref, x1_ref, o_ref, buf_a, buf_b, sem_a, sem_b, *, tm):
    i = pl.program_id(0)
    n = pl.num_programs(0)

    def copies(half_blk, g, u, buf, sem):
        tok = half_blk * tm + g * SUBLANES + u
        return [pltpu.make_async_copy(_row_ref(y4_ref, *_split_row(dest_ref[tok * TOP_K + k])),
                                      _row_ref(buf, k * (tm // SUBLANES) + g, u), sem)
                for k in range(TOP_K)]

    def issue(half_blk, buf, sem):
        def row(g, u):
            for cp in copies(half_blk, g, u, buf, sem):
                cp.start()
        _for_rows(tm, row)

    def drain(half_blk, buf, sem):
        def row(g, u):
            for cp in copies(half_blk, g, u, buf, sem):
                cp.wait()
        _for_rows(tm, row)

    def reduce(buf, row0):
        gates = gate_ref[row0:row0 + tm, :]
        tg = tm // SUBLANES
        for c in range(buf.shape[1]):
            acc = x1_ref[row0:row0 + tm, c * LANES:(c + 1) * LANES]
            for k in range(TOP_K):
                acc = acc + gates[:, k:k + 1] * buf[k * tg:(k + 1) * tg, c].reshape(tm, LANES)
            o_ref[row0:row0 + tm, c * LANES:(c + 1) * LANES] = acc

    @pl.when(i == 0)
    def _():
        issue(0, buf_a, sem_a)

    issue(2 * i + 1, buf_b, sem_b)
    drain(2 * i, buf_a, sem_a)
    reduce(buf_a, 0)

    @pl.when(i < n - 1)
    def _():
        issue(2 * i + 2, buf_a, sem_a)

    drain(2 * i + 1, buf_b, sem_b)
    reduce(buf_b, tm)


def _combine(dest_flat, y4, gates, x1, d: Dims):
    t, dm = x1.shape
    tm = d.tm_comb
    kern = functools.partial(_combine_kernel, tm=tm)
    buf = pltpu.VMEM((TOP_K * tm // SUBLANES,) + y4.shape[1:], F32)
    return pl.pallas_call(
        kern,
        out_shape=jax.ShapeDtypeStruct((t, dm), F32),
        grid_spec=pltpu.PrefetchScalarGridSpec(
            num_scalar_prefetch=1,
            grid=(t // (2 * tm),),
            in_specs=[
                pl.BlockSpec(memory_space=pl.ANY),
                pl.BlockSpec((2 * tm, TOP_K), lambda i, dest: (i, 0)),
                pl.BlockSpec((2 * tm, dm), lambda i, dest: (i, 0)),
            ],
            out_specs=pl.BlockSpec((2 * tm, dm), lambda i, dest: (i, 0)),
            scratch_shapes=[buf, buf, pltpu.SemaphoreType.DMA, pltpu.SemaphoreType.DMA],
        ),
        compiler_params=_cparams(("arbitrary",)),
        name="combine",
    )(dest_flat, y4, gates, x1)


def _pad_lanes(v, fill=0.0):
    v = v.reshape(1, -1).astype(F32)
    return jnp.pad(v, ((0, 0), (0, LANES - v.shape[1])), constant_values=fill)


def _layer(d: Dims, x, positions, g_mix, w_in, g_q_latent, w_uq, g_kv_latent, w_ukv,
           g_q_nope, g_q_rope, g_k_nope, g_k_rope, w_pool, s_pool, w_out,
           g_ffn, w_router, b_router, w_gate_up, b_gate_up, w_down, b_down):
    b, s, dm, h = d.batch, d.seq, d.d_model, d.n_heads
    t = b * s
    assert d.nope == LANES and d.v_dim == LANES and d.rope <= LANES and dm % LANES == 0
    assert d.q_rank == d.kv_rank and d.pool_width % d.q_rank == 0 and dm % d.pool_width == 0
    assert d.tq % d.tk == 0 and s % d.tq == 0

    o_q, o_kv = 0, d.q_rank
    o_kr = o_kv + d.kv_rank
    o_pool = o_kr + d.rope
    o_ga = o_pool + d.pool_width
    o_gp = o_ga + dm
    n_pool = -(-(d.q_rank + d.kv_rank) // d.pool_width) * d.pool_width
    n_ga = -(-(n_pool + d.pool_width) // dm) * dm
    n_gp = n_ga + dm
    n_kr = n_gp + dm
    nz = -(-(n_kr + LANES) // d.tn_in) * d.tn_in
    zeros = lambda n: jnp.zeros((dm, n), BF16)
    w_in_b = w_in.astype(BF16)
    w_in_r = jnp.concatenate([
        w_in_b[:, o_q:o_kr], zeros(n_pool - (d.q_rank + d.kv_rank)),
        w_in_b[:, o_pool:o_ga], zeros(n_ga - n_pool - d.pool_width),
        w_in_b[:, o_ga:o_gp], w_in_b[:, o_gp:],
        w_in_b[:, o_kr:o_pool], zeros(nz - n_kr - d.rope)], axis=1)

    qk = d.nope + d.rope
    half = d.rope // 2
    wq3 = w_uq.reshape(d.q_rank, h, qk)
    wq = jnp.concatenate([wq3, jnp.zeros((d.q_rank, h, 2 * LANES - qk), w_uq.dtype)], axis=2)
    wqt = wq.reshape(d.q_rank, h * 2 * LANES).T.astype(BF16)
    wkv3 = w_ukv.reshape(d.kv_rank, h, d.nope + d.v_dim)
    wk = wkv3[:, :, :d.nope].reshape(d.kv_rank, h * d.nope).astype(BF16)
    wvt = wkv3[:, :, d.nope:].reshape(d.kv_rank, h * d.v_dim).T.astype(BF16)

    inv_freq = ROPE_THETA ** (-np.arange(0, d.rope, 2, dtype=np.float32) / d.rope)
    invf = np.zeros((1, LANES), np.float32)
    invf[0, :half] = inv_freq
    invf[0, half:d.rope] = inv_freq
    sgn = np.zeros((1, LANES), np.float32)
    sgn[0, :half] = -1.0
    sgn[0, half:d.rope] = 1.0
    tmq = d.tk
    invft = np.ascontiguousarray(np.broadcast_to(inv_freq[:, None], (half, tmq)))
    consts = (jnp.asarray(invf), jnp.asarray(sgn), jnp.asarray(invft))
    bcast = lambda v: jnp.broadcast_to(v.astype(F32)[:, None], (v.shape[0], tmq))

    x2d = x.reshape(t, dm)
    pos_f = positions.reshape(t).astype(F32)

    z = _inproj(x2d, g_mix.reshape(1, dm), w_in_r, d)
    qt, k, vt = _qkv(z, pos_f.reshape(t, 1), pos_f.reshape(1, t), consts,
                     g_q_latent.reshape(1, -1), g_kv_latent.reshape(1, -1),
                     bcast(g_q_nope), bcast(g_q_rope[:half]), bcast(g_q_rope[half:]),
                     _pad_lanes(g_k_nope), _pad_lanes(g_k_rope), wqt, wk, wvt, d, n_kr)
    ya, wgu_bf = _attn(qt, k, vt, w_gate_up.reshape(d.n_experts * dm, 2 * d.d_ff), d)
    wr = jnp.pad(w_router.astype(F32), ((0, 0), (0, LANES - d.n_experts)))
    wr_hi = wr.astype(BF16)
    wr = jnp.concatenate([wr_hi, (wr - wr_hi.astype(F32)).astype(BF16)], axis=1)
    x1, h4, logits = _mixout(z, ya, x2d, w_pool.astype(BF16), s_pool.reshape(1, dm), w_out.astype(BF16),
                             g_ffn.reshape(1, dm), wr, _pad_lanes(b_router), d, n_pool, n_ga, n_gp)

    dest, gates, counts = _route(logits, d)
    tm = d.tm_moe
    n_rows = t * TOP_K + d.n_experts * tm
    n_tiles = n_rows // tm
    cnt = counts[0, :d.n_experts].astype(jnp.int32)
    ends = jnp.cumsum((cnt + tm - 1) // tm * tm)
    n_used = ends[-1] // tm
    tile_ids = jnp.minimum(jnp.arange(n_tiles, dtype=jnp.int32), n_used - 1)
    tile_expert = jnp.sum(ends[None, :] <= (tile_ids * tm)[:, None], axis=1).astype(jnp.int32)
    tile_expert = jnp.minimum(tile_expert, d.n_experts - 1)
    starts = ends - (cnt + tm - 1) // tm * tm
    tile_valid = jnp.clip(starts[tile_expert] + cnt[tile_expert] - tile_ids * tm, 0, tm).astype(jnp.int32)

    dest_flat = dest.reshape(t * TOP_K)
    xs4 = _dispatch(dest_flat, h4, n_rows, d)
    y4 = _moe(tile_expert, tile_ids, tile_valid, n_used.reshape(1).astype(jnp.int32), xs4,
              wgu_bf, b_gate_up.reshape(d.n_experts, 1, 2 * d.d_ff),
              w_down, b_down.reshape(d.n_experts, 1, dm), d)
    out = _combine(dest_flat, y4, gates, x1, d)
    return out.reshape(b, s, dm)


def kernel(x, positions, g_mix, w_in, g_q_latent, w_uq, g_kv_latent, w_ukv, g_q_nope, g_q_rope, g_k_nope, g_k_rope, w_pool, s_pool, w_out, g_ffn, w_router, b_router, w_gate_up, b_gate_up, w_down, b_down):
    return _layer(Dims(), x, positions, g_mix, w_in, g_q_latent, w_uq, g_kv_latent, w_ukv,
                  g_q_nope, g_q_rope, g_k_nope, g_k_rope, w_pool, s_pool, w_out,
                  g_ffn, w_router, b_router, w_gate_up, b_gate_up, w_down, b_down)
```

```python
import functools
import math
from typing import NamedTuple

import numpy as np
import jax
import jax.numpy as jnp
from jax import lax
from jax.experimental import pallas as pl
from jax.experimental.pallas import tpu as pltpu

F32 = jnp.float32
BF16 = jnp.bfloat16

RMS_EPS = 1e-6
NEG_INF = -1e30
ROPE_THETA = 10000.0
SWIGLU_LIMIT = 7.0
SWIGLU_ALPHA = 1.702
POOL_WINDOWS = (2, 4, 8, 16)
TOP_K = 4

LANES = 128
SUBLANES = 8
POOL_HALO = 16
VMEM_LIMIT = 62 * 1024 * 1024


class Dims(NamedTuple):
    batch: int = 2
    seq: int = 8192
    d_model: int = 2048
    n_heads: int = 16
    nope: int = 128
    rope: int = 64
    v_dim: int = 128
    q_rank: int = 512
    kv_rank: int = 512
    pool_width: int = 1024
    n_experts: int = 32
    d_ff: int = 2048
    tm_in: int = 1024
    tn_in: int = 1664
    tq: int = 2048
    tk: int = 512
    attn_gw: int = 512
    tm_mix: int = 256
    tm_route: int = 512
    tm_disp: int = 256
    tm_moe: int = 512
    tf_moe: int = 1024
    tm_comb: int = 128


def _cparams(sem, vmem=VMEM_LIMIT):
    return pltpu.CompilerParams(dimension_semantics=sem, vmem_limit_bytes=vmem)


def _const_spec(shape):
    nd = len(shape)
    return pl.BlockSpec(shape, lambda *_: (0,) * nd, pipeline_mode=pl.Buffered(1))


def _rms(t, gain):
    return t * lax.rsqrt(jnp.mean(t * t, axis=-1, keepdims=True) + RMS_EPS) * gain


def _sigmoid(t):
    return 0.5 * jnp.tanh(0.5 * t) + 0.5


def _row_tiles(v):
    return v.reshape(v.shape[0] // SUBLANES, SUBLANES, LANES)


def _inproj_kernel(x_ref, g_ref, w_ref, z_ref, h_scr):
    @pl.when(pl.program_id(1) == 0)
    def _():
        h_scr[...] = _rms(x_ref[...], g_ref[...]).astype(BF16)

    z_ref[...] = jnp.dot(h_scr[...], w_ref[...], preferred_element_type=F32).astype(z_ref.dtype)


def _inproj(x2d, g_mix, w_in_r, d: Dims):
    t, dm = x2d.shape
    nz = w_in_r.shape[1]
    tm, tn = d.tm_in, d.tn_in
    return pl.pallas_call(
        _inproj_kernel,
        out_shape=jax.ShapeDtypeStruct((t, nz), BF16),
        grid=(t // tm, nz // tn),
        in_specs=[
            pl.BlockSpec((tm, dm), lambda i, j: (i, 0)),
            pl.BlockSpec((1, dm), lambda i, j: (0, 0)),
            pl.BlockSpec((dm, tn), lambda i, j: (0, j)),
        ],
        out_specs=pl.BlockSpec((tm, tn), lambda i, j: (i, j)),
        scratch_shapes=[pltpu.VMEM((tm, dm), BF16)],
        compiler_params=_cparams(("parallel", "arbitrary")),
        name="inproj",
    )(x2d, g_mix, w_in_r)


def _rope_tile(t, cosf, sinf, half):
    lane = lax.broadcasted_iota(jnp.int32, t.shape, 1)
    swapped = jnp.where(lane < half, pltpu.roll(t, LANES - half, 1), pltpu.roll(t, half, 1))
    return t * cosf + swapped * sinf


def _qkv_kernel(zq_ref, zkv_ref, zkr_ref, pos_ref, posr_ref, invf_ref, sgn_ref, invft_ref,
                gql_ref, gkvl_ref, gqnt_ref, gqr1_ref, gqr2_ref, gkn_ref, gkr_ref,
                wqt_ref, wk_ref, wvt_ref,
                qt_ref, k_ref, vt_ref, *, n_heads, rope, qscale):
    half = rope // 2
    hw = 2 * LANES
    ang = pos_ref[...] * invf_ref[...]
    cosf = jnp.cos(ang)
    sinf = jnp.sin(ang) * sgn_ref[...]
    angt = invft_ref[...] * posr_ref[...]
    cost = jnp.cos(angt)
    sint = jnp.sin(angt)

    hq = _rms(zq_ref[...].astype(F32), gql_ref[...])
    hkv = _rms(zkv_ref[...].astype(F32), gkvl_ref[...])
    hqt = hq.T.astype(BF16)
    hkvt = hkv.T.astype(BF16)
    hkv = hkv.astype(BF16)

    kr = zkr_ref[...].astype(F32)
    kr = kr * lax.rsqrt(jnp.sum(kr * kr, axis=-1, keepdims=True) / rope + RMS_EPS) * gkr_ref[...]
    kpe = _rope_tile(kr, cosf, sinf, half).astype(BF16)

    for h in range(n_heads):
        qh = jnp.dot(wqt_ref[h * hw:(h + 1) * hw, :], hqt, preferred_element_type=F32)
        qn = qh[:LANES]
        qn = qn * lax.rsqrt(jnp.mean(qn * qn, axis=0, keepdims=True) + RMS_EPS) * gqnt_ref[...]
        t1 = qh[LANES:LANES + half]
        t2 = qh[LANES + half:LANES + rope]
        ss = jnp.sum(t1 * t1, axis=0, keepdims=True) + jnp.sum(t2 * t2, axis=0, keepdims=True)
        r = lax.rsqrt(ss / rope + RMS_EPS)
        t1 = t1 * r * gqr1_ref[...]
        t2 = t2 * r * gqr2_ref[...]
        qt_ref[h * hw:h * hw + LANES, :] = (qn * qscale).astype(BF16)
        qt_ref[h * hw + LANES:h * hw + LANES + half, :] = ((t1 * cost - t2 * sint) * qscale).astype(BF16)
        qt_ref[h * hw + LANES + half:h * hw + LANES + rope, :] = ((t2 * cost + t1 * sint) * qscale).astype(BF16)
        qt_ref[h * hw + LANES + rope:(h + 1) * hw, :] = jnp.zeros((LANES - rope, qt_ref.shape[1]), BF16)

        kh = jnp.dot(hkv, wk_ref[:, h * LANES:(h + 1) * LANES], preferred_element_type=F32)
        k_ref[:, h * hw:h * hw + LANES] = _rms(kh, gkn_ref[...]).astype(BF16)
        k_ref[:, h * hw + LANES:(h + 1) * hw] = kpe

    for c in range(wvt_ref.shape[0] // hw):
        vt = jnp.dot(wvt_ref[c * hw:(c + 1) * hw, :], hkvt, preferred_element_type=F32)
        vt_ref[0, c * hw:(c + 1) * hw, :] = vt.astype(BF16)


def _qkv(z, pos_col, pos_row, consts, gql, gkvl, gqnt, gqr1, gqr2, gkn, gkr, wqt, wk, wvt, d: Dims, off_kr):
    t = z.shape[0]
    tm = d.tk
    invf, sgn, invft = consts
    hq = d.n_heads * 2 * LANES
    hv = d.n_heads * d.v_dim
    row = lambda a: pl.BlockSpec(a.shape, lambda i: (0, 0))
    kern = functools.partial(_qkv_kernel, n_heads=d.n_heads, rope=d.rope,
                             qscale=math.log2(math.e) / math.sqrt(d.nope + d.rope))
    return pl.pallas_call(
        kern,
        out_shape=(jax.ShapeDtypeStruct((hq, t), BF16),
                   jax.ShapeDtypeStruct((t, hq), BF16),
                   jax.ShapeDtypeStruct((t // tm, hv, tm), BF16)),
        grid=(t // tm,),
        in_specs=[
            pl.BlockSpec((tm, d.q_rank), lambda i: (i, 0)),
            pl.BlockSpec((tm, d.kv_rank), lambda i: (i, d.q_rank // d.kv_rank)),
            pl.BlockSpec((tm, LANES), lambda i: (i, off_kr // LANES)),
            pl.BlockSpec((tm, 1), lambda i: (i, 0)),
            pl.BlockSpec((1, tm), lambda i: (0, i)),
            row(invf), row(sgn), row(invft),
            row(gql), row(gkvl), row(gqnt), row(gqr1), row(gqr2), row(gkn), row(gkr),
            _const_spec(wqt.shape), _const_spec(wk.shape), _const_spec(wvt.shape),
        ],
        out_specs=(pl.BlockSpec((hq, tm), lambda i: (0, i)),
                   pl.BlockSpec((tm, hq), lambda i: (i, 0)),
                   pl.BlockSpec((1, hv, tm), lambda i: (i, 0, 0))),
        compiler_params=_cparams(("parallel",)),
        name="qkv",
    )(z, z, z, pos_col, pos_row, invf, sgn, invft, gql, gkvl, gqnt, gqr1, gqr2, gkn, gkr, wqt, wk, wvt)


def _attn_kernel(qt_ref, k_ref, vt_ref, wgu_ref, wd_ref, o_ref, wgu_bf_ref, wd_bf_ref, *scr, tq, tk, gw):
    i = pl.program_id(2)
    n_sub = tq // tk
    ng = tq // gw
    cw = wgu_bf_ref.shape[3]
    for c in range(wgu_bf_ref.shape[1]):
        wgu_bf_ref[0, c] = wgu_ref[:, c * cw:(c + 1) * cw].astype(BF16)
    wd_bf_ref[...] = wd_ref[...].astype(BF16)
    m_scr, l_scr, acc_scr = scr[:ng], scr[ng:2 * ng], scr[2 * ng:3 * ng]
    s_scr, cm_scr = scr[3 * ng:4 * ng], scr[4 * ng:]
    for g in range(ng):
        m_scr[g][...] = jnp.full(m_scr[g].shape, NEG_INF, F32)
        l_scr[g][...] = jnp.zeros(l_scr[g].shape, F32)
        acc_scr[g][...] = jnp.zeros(acc_scr[g].shape, F32)

    def scores(c, g):
        kc = k_ref[pl.ds(pl.multiple_of(c * tk, tk), tk), :]
        return jnp.dot(kc, qt_ref[:, g * gw:(g + 1) * gw], preferred_element_type=F32)

    def softmax_pv(c, g, s, cmax):
        m_old = m_scr[g][...]
        m_new = jnp.maximum(m_old, cmax)
        p = jnp.exp2(s - m_new)
        alpha = jnp.exp2(m_old - m_new)
        l_scr[g][...] = alpha * l_scr[g][...] + jnp.sum(p, axis=0, keepdims=True)
        pv = jnp.dot(vt_ref[c], p.astype(BF16), preferred_element_type=F32)
        acc_scr[g][...] = alpha * acc_scr[g][...] + pv
        m_scr[g][...] = m_new

    def stash(g, s):
        s_scr[g][...] = s
        cm_scr[g][...] = jnp.max(s, axis=0, keepdims=True)

    n_full = i * n_sub

    @pl.when(n_full > 0)
    def _():
        for g in range(ng):
            stash(g, scores(0, g))

        def body(c, carry):
            nxt = [scores(c + 1, g) for g in range(ng)]
            for g in range(ng):
                softmax_pv(c, g, s_scr[g][...], cm_scr[g][...])
            for g in range(ng):
                stash(g, nxt[g])
            return carry

        lax.fori_loop(0, n_full - 1, body, 0)
        for g in range(ng):
            softmax_pv(n_full - 1, g, s_scr[g][...], cm_scr[g][...])

    key = lax.broadcasted_iota(jnp.int32, (tk, gw), 0)
    qry = lax.broadcasted_iota(jnp.int32, (tk, gw), 1)
    steps = []
    for cc in range(n_sub):
        for g in range(ng):
            k_lo, k_hi = cc * tk, (cc + 1) * tk - 1
            q_lo, q_hi = g * gw, (g + 1) * gw - 1
            if k_lo > q_hi:
                continue
            steps.append((cc, g, k_hi > q_lo, k_lo, q_lo))

    def diag_scores(step):
        cc, g, masked, k_lo, q_lo = step
        s = scores(n_full + cc, g)
        if masked:
            s = jnp.where((key + k_lo) <= (qry + q_lo), s, NEG_INF)
        return s

    s_cur = diag_scores(steps[0])
    for j, step in enumerate(steps):
        s_nxt = diag_scores(steps[j + 1]) if j + 1 < len(steps) else None
        softmax_pv(n_full + step[0], step[1], s_cur, jnp.max(s_cur, axis=0, keepdims=True))
        s_cur = s_nxt

    for g in range(ng):
        o = acc_scr[g][...] / l_scr[g][...]
        o_ref[g * gw:(g + 1) * gw, :] = o.T.astype(o_ref.dtype)


def _attn(qt, k, vt, wgu2d, wd2d, d: Dims):
    b, s, h = d.batch, d.seq, d.n_heads
    tq, tk = d.tq, d.tk
    gw = min(d.attn_gw, tq)
    ng = tq // gw
    nq = s // tq
    n_steps = b * h * nq
    ru = wgu2d.shape[0] // n_steps
    dm, tf = d.d_model, d.tf_moe
    spe = dm // ru
    ncol = wgu2d.shape[1] // tf
    rd = wd2d.shape[0] // n_steps
    assert ru * n_steps == wgu2d.shape[0] and spe * ru == dm and rd * n_steps == wd2d.shape[0]
    step = lambda bi, hi, i: ((bi * h + hi) * nq + i, 0)
    kern = functools.partial(_attn_kernel, tq=tq, tk=tk, gw=gw)
    return pl.pallas_call(
        kern,
        out_shape=(jax.ShapeDtypeStruct((b * s, h * d.v_dim), BF16),
                   jax.ShapeDtypeStruct((d.n_experts, ncol, dm, tf), BF16),
                   jax.ShapeDtypeStruct(wd2d.shape, BF16)),
        grid=(b, h, nq),
        in_specs=[
            pl.BlockSpec((2 * LANES, tq), lambda bi, hi, i: (hi, bi * nq + i)),
            pl.BlockSpec((s, 2 * LANES), lambda bi, hi, i: (bi, hi)),
            pl.BlockSpec((s // tk, d.v_dim, tk), lambda bi, hi, i: (bi, hi, 0)),
            pl.BlockSpec((ru, wgu2d.shape[1]), step),
            pl.BlockSpec((rd, wd2d.shape[1]), step),
        ],
        out_specs=(pl.BlockSpec((tq, d.v_dim), lambda bi, hi, i: (bi * nq + i, hi)),
                   pl.BlockSpec((1, ncol, ru, tf),
                                lambda bi, hi, i: (step(bi, hi, i)[0] // spe, 0, step(bi, hi, i)[0] % spe, 0)),
                   pl.BlockSpec((rd, wd2d.shape[1]), step)),
        scratch_shapes=([pltpu.VMEM((1, gw), F32)] * (2 * ng) + [pltpu.VMEM((d.v_dim, gw), F32)] * ng
                        + [pltpu.VMEM((tk, gw), F32)] * ng + [pltpu.VMEM((1, gw), F32)] * ng),
        compiler_params=_cparams(("parallel", "parallel", "arbitrary")),
        name="attn",
    )(qt, k, vt, wgu2d, wd2d)


def _mixout_kernel(u_ref, halo_ref, ga_ref, gp_ref, ya_ref, x_ref,
                   wpool_ref, spool_ref, wout_ref, gffn_ref, wr_ref, br_ref,
                   x1_ref, h4_ref, lg_ref, merged_scr, *, tm, seq, n_groups):
    i = pl.program_id(0)
    row0 = (i * tm) % seq
    gdim = u_ref.shape[1] // n_groups
    odim = ga_ref.shape[1] // n_groups
    pos = row0 + lax.broadcasted_iota(jnp.int32, (tm, 1), 0)
    halo_on = (row0 > 0).astype(F32)

    for g in range(n_groups):
        w = POOL_WINDOWS[g]
        ug = u_ref[:, g * gdim:(g + 1) * gdim].astype(F32)
        hg = halo_ref[:, g * gdim:(g + 1) * gdim].astype(F32) * halo_on
        ext = jnp.concatenate([hg, ug], axis=0)
        shift = 1
        while shift < w:
            ext = ext + pltpu.roll(ext, shift, 0)
            shift *= 2
        wsum = ext[POOL_HALO:, :]
        count = jnp.minimum(pos + 1, w).astype(F32)
        pooled = (wsum / count - ug).astype(BF16)
        yp = jnp.dot(pooled, wpool_ref[g], preferred_element_type=F32)
        yp = yp * spool_ref[:, g * odim:(g + 1) * odim]
        sl = slice(g * odim, (g + 1) * odim)
        merged = (_sigmoid(ga_ref[:, sl].astype(F32)) * ya_ref[:, sl].astype(F32)
                  + _sigmoid(gp_ref[:, sl].astype(F32)) * yp)
        merged_scr[:, sl] = merged.astype(BF16)

    x1 = x_ref[...] + jnp.dot(merged_scr[...], wout_ref[...], preferred_element_type=F32)
    x1_ref[...] = x1
    hn = _rms(x1, gffn_ref[...])
    hi = hn.astype(BF16)
    lo = (hn - hi.astype(F32)).astype(BF16)
    r = jnp.dot(jnp.concatenate([hi, lo], axis=0), wr_ref[...], preferred_element_type=F32)
    lg_ref[...] = (r[:tm, :LANES] + r[:tm, LANES:]) + (r[tm:, :LANES] + r[tm:, LANES:]) + br_ref[...]
    for c in range(h4_ref.shape[1]):
        h4_ref[:, c] = _row_tiles(hn[:, c * LANES:(c + 1) * LANES])


def _mixout(z, ya, x2d, wpool, spool, wout, gffn, wr, br, d: Dims, off_pool, off_ga, off_gp):
    t, dm = x2d.shape
    tm = d.tm_mix
    pw = d.pool_width
    n_groups = len(POOL_WINDOWS)
    hb = tm // POOL_HALO
    nc = dm // LANES
    kern = functools.partial(_mixout_kernel, tm=tm, seq=d.seq, n_groups=n_groups)
    return pl.pallas_call(
        kern,
        out_shape=(jax.ShapeDtypeStruct((t, dm), F32),
                   jax.ShapeDtypeStruct((t // SUBLANES, nc, SUBLANES, LANES), F32),
                   jax.ShapeDtypeStruct((t, LANES), F32)),
        grid=(t // tm,),
        in_specs=[
            pl.BlockSpec((tm, pw), lambda i: (i, off_pool // pw)),
            pl.BlockSpec((POOL_HALO, pw), lambda i: (jnp.maximum(i * hb - 1, 0), off_pool // pw)),
            pl.BlockSpec((tm, dm), lambda i: (i, off_ga // dm)),
            pl.BlockSpec((tm, dm), lambda i: (i, off_gp // dm)),
            pl.BlockSpec((tm, dm), lambda i: (i, 0)),
            pl.BlockSpec((tm, dm), lambda i: (i, 0)),
            _const_spec(wpool.shape), _const_spec(spool.shape), _const_spec(wout.shape),
            _const_spec(gffn.shape), _const_spec(wr.shape), _const_spec(br.shape),
        ],
        out_specs=(pl.BlockSpec((tm, dm), lambda i: (i, 0)),
                   pl.BlockSpec((tm // SUBLANES, nc, SUBLANES, LANES), lambda i: (i, 0, 0, 0)),
                   pl.BlockSpec((tm, LANES), lambda i: (i, 0))),
        scratch_shapes=[pltpu.VMEM((tm, dm), BF16)],
        compiler_params=_cparams(("parallel",)),
        name="mixout",
    )(z, z, z, z, ya, x2d, wpool, spool, wout, gffn, wr, br)


def _lane_excl_cumsum(v):
    lane = lax.broadcasted_iota(jnp.int32, v.shape, 1)
    inc = v
    shift = 1
    while shift < LANES:
        inc = inc + jnp.where(lane >= shift, pltpu.roll(inc, shift, 1), 0.0)
        shift *= 2
    return inc - v


def _route_kernel(lg_ref, dest_ref, gate_ref, cnt_ref, carry_scr, base_scr, *, tm, n_experts, tile):
    phase = pl.program_id(0)
    i = pl.program_id(1)
    lane = lax.broadcasted_iota(jnp.int32, (tm, LANES), 1)
    work = jnp.where(lane < n_experts, lg_ref[...], -jnp.inf)

    vals, hots = [], []
    for _ in range(TOP_K):
        m = jnp.max(work, axis=-1, keepdims=True)
        first = jnp.min(jnp.where(work == m, lane, LANES), axis=-1, keepdims=True)
        hot = lane == first
        vals.append(m)
        hots.append(hot)
        work = jnp.where(hot, -jnp.inf, work)
    chosen = jnp.zeros((tm, LANES), F32)
    for hot in hots:
        chosen = chosen + hot.astype(F32)

    @pl.when((phase == 0) & (i == 0))
    def _():
        carry_scr[...] = jnp.zeros(carry_scr.shape, F32)

    @pl.when((phase == 1) & (i == 0))
    def _():
        counts = carry_scr[...]
        padded = jnp.ceil(counts / tile) * tile
        base_scr[...] = _lane_excl_cumsum(padded)
        cnt_ref[...] = counts
        carry_scr[...] = jnp.zeros(carry_scr.shape, F32)

    @pl.when(phase == 1)
    def _():
        r = lax.broadcasted_iota(jnp.int32, (tm, tm), 0)
        c = lax.broadcasted_iota(jnp.int32, (tm, tm), 1)
        tri = (c < r).astype(BF16)
        before = jnp.dot(tri, chosen.astype(BF16), preferred_element_type=F32)
        slot = before + carry_scr[...] + base_scr[...]
        exps = [jnp.exp(v - vals[0]) for v in vals]
        den = exps[0] + exps[1] + exps[2] + exps[3]
        for k in range(TOP_K):
            dk = jnp.sum(jnp.where(hots[k], slot, 0.0), axis=-1, keepdims=True)
            dest_ref[:, k:k + 1] = dk.astype(jnp.int32)
            gate_ref[:, k:k + 1] = exps[k] / den

    carry_scr[...] = carry_scr[...] + jnp.sum(chosen, axis=0, keepdims=True)


def _route(logits, d: Dims):
    t = logits.shape[0]
    tm = d.tm_route
    kern = functools.partial(_route_kernel, tm=tm, n_experts=d.n_experts, tile=d.tm_moe)
    return pl.pallas_call(
        kern,
        out_shape=(jax.ShapeDtypeStruct((t, TOP_K), jnp.int32),
                   jax.ShapeDtypeStruct((t, TOP_K), F32),
                   jax.ShapeDtypeStruct((1, LANES), F32)),
        grid=(2, t // tm),
        in_specs=[pl.BlockSpec((tm, LANES), lambda p, i: (i, 0))],
        out_specs=(pl.BlockSpec((tm, TOP_K), lambda p, i: (i * p, 0)),
                   pl.BlockSpec((tm, TOP_K), lambda p, i: (i * p, 0)),
                   pl.BlockSpec((1, LANES), lambda p, i: (0, 0))),
        scratch_shapes=[pltpu.VMEM((1, LANES), F32), pltpu.VMEM((1, LANES), F32)],
        compiler_params=_cparams(("arbitrary", "arbitrary")),
        name="route",
    )(logits)


def _row_ref(ref, group, sub):
    return ref.at[group, :, pl.ds(sub, 1), :]


def _split_row(row):
    return row >> (SUBLANES.bit_length() - 1), row & (SUBLANES - 1)


def _for_rows(n, fn):
    def body(g, carry):
        for u in range(SUBLANES):
            fn(g, u)
        return carry
    lax.fori_loop(0, n // SUBLANES, body, 0)


def _dispatch_kernel(dest_ref, h4_ref, xs_ref, sem, *, tm):
    i = pl.program_id(0)

    def copies(g, u):
        tok = i * tm + g * SUBLANES + u
        src = _row_ref(h4_ref, g, u)
        return [pltpu.make_async_copy(src, _row_ref(xs_ref, *_split_row(dest_ref[tok * TOP_K + k])), sem)
                for k in range(TOP_K)]

    def issue(g, u):
        for cp in copies(g, u):
            cp.start()

    def drain(g, u):
        for cp in copies(g, u):
            cp.wait()

    _for_rows(tm, issue)
    _for_rows(tm, drain)


def _dispatch(dest_flat, h4, n_rows, d: Dims):
    t = h4.shape[0] * SUBLANES
    tm = d.tm_disp
    kern = functools.partial(_dispatch_kernel, tm=tm)
    return pl.pallas_call(
        kern,
        out_shape=jax.ShapeDtypeStruct((n_rows // SUBLANES,) + h4.shape[1:], F32),
        grid_spec=pltpu.PrefetchScalarGridSpec(
            num_scalar_prefetch=1,
            grid=(t // tm,),
            in_specs=[pl.BlockSpec((tm // SUBLANES,) + h4.shape[1:], lambda i, dest: (i, 0, 0, 0))],
            out_specs=pl.BlockSpec(memory_space=pl.ANY),
            scratch_shapes=[pltpu.SemaphoreType.DMA],
        ),
        compiler_params=_cparams(("arbitrary",)),
        name="dispatch",
    )(dest_flat, h4)


def _moe_kernel(te_ref, ts_ref, tv_ref, nu_ref,
                xs_ref, wg_ref, wu_ref, bg_ref, bu_ref, wd_ref, bd_ref,
                y_ref, xb_scr, *, tm):
    i = pl.program_id(0)
    j = pl.program_id(1)
    nc = xs_ref.shape[1]

    @pl.when(i < nu_ref[0])
    def _():
        @pl.when(j == 0)
        def _():
            valid = _row_tiles(lax.broadcasted_iota(jnp.int32, (tm, LANES), 0)) < tv_ref[i]
            for c in range(nc):
                xb_scr[:, c * LANES:(c + 1) * LANES] = (
                    jnp.where(valid, xs_ref[:, c], 0.0).reshape(tm, LANES).astype(BF16))
                y_ref[:, c] = jnp.broadcast_to(bd_ref[0, :, c * LANES:(c + 1) * LANES],
                                               (tm // SUBLANES, SUBLANES, LANES))

        def ffn(rows):
            tg = rows // SUBLANES
            x = xb_scr[:rows, :]
            gate = jnp.dot(x, wg_ref[0, 0], preferred_element_type=F32) + bg_ref[0]
            up = jnp.dot(x, wu_ref[0, 0], preferred_element_type=F32) + bu_ref[0]
            gate = jnp.minimum(gate, SWIGLU_LIMIT)
            up = jnp.clip(up, -SWIGLU_LIMIT, SWIGLU_LIMIT)
            act = ((up + 1.0) * gate * _sigmoid(SWIGLU_ALPHA * gate)).astype(BF16)
            for c in range(0, nc, 2):
                part = jnp.dot(act, wd_ref[0, :, c * LANES:(c + 2) * LANES], preferred_element_type=F32)
                y_ref[:tg, c] = y_ref[:tg, c] + _row_tiles(part[:, :LANES])
                y_ref[:tg, c + 1] = y_ref[:tg, c + 1] + _row_tiles(part[:, LANES:])

        @pl.when(tv_ref[i] > tm // 2)
        def _():
            ffn(tm)

        @pl.when(tv_ref[i] <= tm // 2)
        def _():
            ffn(tm // 2)


def _moe(tile_expert, tile_src, tile_valid, n_used, xs4, wgu, bgu, wd, bd, d: Dims):
    n_rows = xs4.shape[0] * SUBLANES
    tm, tf = d.tm_moe, d.tf_moe
    dm, ff = d.d_model, d.d_ff
    nj = ff // tf
    n_tiles = n_rows // tm
    blk4 = (tm // SUBLANES,) + xs4.shape[1:]

    def jj(i, j, nu):
        return jnp.where(i < nu[0], j, nj - 1)

    kern = functools.partial(_moe_kernel, tm=tm)
    return pl.pallas_call(
        kern,
        out_shape=jax.ShapeDtypeStruct(xs4.shape, F32),
        grid_spec=pltpu.PrefetchScalarGridSpec(
            num_scalar_prefetch=4,
            grid=(n_tiles, nj),
            in_specs=[
                pl.BlockSpec(blk4, lambda i, j, te, ts, tv, nu: (ts[i], 0, 0, 0)),
                pl.BlockSpec((1, 1, dm, tf), lambda i, j, te, ts, tv, nu: (te[i], jj(i, j, nu), 0, 0)),
                pl.BlockSpec((1, 1, dm, tf), lambda i, j, te, ts, tv, nu: (te[i], nj + jj(i, j, nu), 0, 0)),
                pl.BlockSpec((1, 1, tf), lambda i, j, te, ts, tv, nu: (te[i], 0, jj(i, j, nu))),
                pl.BlockSpec((1, 1, tf), lambda i, j, te, ts, tv, nu: (te[i], 0, nj + jj(i, j, nu))),
                pl.BlockSpec((1, tf, dm), lambda i, j, te, ts, tv, nu: (te[i], jj(i, j, nu), 0)),
                pl.BlockSpec((1, 1, dm), lambda i, j, te, ts, tv, nu: (te[i], 0, 0)),
            ],
            out_specs=pl.BlockSpec(blk4, lambda i, j, te, ts, tv, nu: (ts[i], 0, 0, 0)),
            scratch_shapes=[pltpu.VMEM((tm, dm), BF16)],
        ),
        compiler_params=_cparams(("arbitrary", "arbitrary")),
        name="moe",
    )(tile_expert, tile_src, tile_valid, n_used, xs4, wgu, wgu, bgu, bgu, wd, bd)


def _combine_kernel(dest_ref, y4_ref, gate_ref, x1_ref, o_ref, buf_a, buf_b, sem_a, sem_b, *, tm):
    i = pl.program_id(0)
    n = pl.num_programs(0)

    def copies(half_blk, g, u, buf, sem):
        tok = half_blk * tm + g * SUBLANES + u
        return [pltpu.make_async_copy(_row_ref(y4_ref, *_split_row(dest_ref[tok * TOP_K + k])),
                                      _row_ref(buf, k * (tm // SUBLANES) + g, u), sem)
                for k in range(TOP_K)]

    def issue(half_blk, buf, sem):
        def row(g, u):
            for cp in copies(half_blk, g, u, buf, sem):
                cp.start()
        _for_rows(tm, row)

    def drain(half_blk, buf, sem):
        def row(g, u):
            for cp in copies(half_blk, g, u, buf, sem):
                cp.wait()
        _for_rows(tm, row)

    def reduce(buf, row0):
        gates = gate_ref[row0:row0 + tm, :]
        tg = tm // SUBLANES
        for c in range(buf.shape[1]):
            acc = x1_ref[row0:row0 + tm, c * LANES:(c + 1) * LANES]
            for k in range(TOP_K):
                acc = acc + gates[:, k:k + 1] * buf[k * tg:(k + 1) * tg, c].reshape(tm, LANES)
            o_ref[row0:row0 + tm, c * LANES:(c + 1) * LANES] = acc

    @pl.when(i == 0)
    def _():
        issue(0, buf_a, sem_a)

    issue(2 * i + 1, buf_b, sem_b)
    drain(2 * i, buf_a, sem_a)
    reduce(buf_a, 0)

    @pl.when(i < n - 1)
    def _():
        issue(2 * i + 2, buf_a, sem_a)

    drain(2 * i + 1, buf_b, sem_b)
    reduce(buf_b, tm)


def _combine(dest_flat, y4, gates, x1, d: Dims):
    t, dm = x1.shape
    tm = d.tm_comb
    kern = functools.partial(_combine_kernel, tm=tm)
    buf = pltpu.VMEM((TOP_K * tm // SUBLANES,) + y4.shape[1:], F32)
    return pl.pallas_call(
        kern,
        out_shape=jax.ShapeDtypeStruct((t, dm), F32),
        grid_spec=pltpu.PrefetchScalarGridSpec(
            num_scalar_prefetch=1,
            grid=(t // (2 * tm),),
            in_specs=[
                pl.BlockSpec(memory_space=pl.ANY),
                pl.BlockSpec((2 * tm, TOP_K), lambda i, dest: (i, 0)),
                pl.BlockSpec((2 * tm, dm), lambda i, dest: (i, 0)),
            ],
            out_specs=pl.BlockSpec((2 * tm, dm), lambda i, dest: (i, 0)),
            scratch_shapes=[buf, buf, pltpu.SemaphoreType.DMA, pltpu.SemaphoreType.DMA],
        ),
        compiler_params=_cparams(("arbitrary",)),
        name="combine",
    )(dest_flat, y4, gates, x1)


def _pad_lanes(v, fill=0.0):
    v = v.reshape(1, -1).astype(F32)
    return jnp.pad(v, ((0, 0), (0, LANES - v.shape[1])), constant_values=fill)


def _layer(d: Dims, x, positions, g_mix, w_in, g_q_latent, w_uq, g_kv_latent, w_ukv,
           g_q_nope, g_q_rope, g_k_nope, g_k_rope, w_pool, s_pool, w_out,
           g_ffn, w_router, b_router, w_gate_up, b_gate_up, w_down, b_down):
    b, s, dm, h = d.batch, d.seq, d.d_model, d.n_heads
    t = b * s
    assert d.nope == LANES and d.v_dim == LANES and d.rope <= LANES and dm % LANES == 0
    assert d.q_rank == d.kv_rank and d.pool_width % d.q_rank == 0 and dm % d.pool_width == 0
    assert d.tq % d.tk == 0 and s % d.tq == 0

    o_q, o_kv = 0, d.q_rank
    o_kr = o_kv + d.kv_rank
    o_pool = o_kr + d.rope
    o_ga = o_pool + d.pool_width
    o_gp = o_ga + dm
    n_pool = -(-(d.q_rank + d.kv_rank) // d.pool_width) * d.pool_width
    n_ga = -(-(n_pool + d.pool_width) // dm) * dm
    n_gp = n_ga + dm
    n_kr = n_gp + dm
    nz = -(-(n_kr + LANES) // d.tn_in) * d.tn_in
    zeros = lambda n: jnp.zeros((dm, n), BF16)
    w_in_b = w_in.astype(BF16)
    w_in_r = jnp.concatenate([
        w_in_b[:, o_q:o_kr], zeros(n_pool - (d.q_rank + d.kv_rank)),
        w_in_b[:, o_pool:o_ga], zeros(n_ga - n_pool - d.pool_width),
        w_in_b[:, o_ga:o_gp], w_in_b[:, o_gp:],
        w_in_b[:, o_kr:o_pool], zeros(nz - n_kr - d.rope)], axis=1)

    qk = d.nope + d.rope
    half = d.rope // 2
    wq3 = w_uq.reshape(d.q_rank, h, qk)
    wq = jnp.concatenate([wq3, jnp.zeros((d.q_rank, h, 2 * LANES - qk), w_uq.dtype)], axis=2)
    wqt = wq.reshape(d.q_rank, h * 2 * LANES).T.astype(BF16)
    wkv3 = w_ukv.reshape(d.kv_rank, h, d.nope + d.v_dim)
    wk = wkv3[:, :, :d.nope].reshape(d.kv_rank, h * d.nope).astype(BF16)
    wvt = wkv3[:, :, d.nope:].reshape(d.kv_rank, h * d.v_dim).T.astype(BF16)

    inv_freq = ROPE_THETA ** (-np.arange(0, d.rope, 2, dtype=np.float32) / d.rope)
    invf = np.zeros((1, LANES), np.float32)
    invf[0, :half] = inv_freq
    invf[0, half:d.rope] = inv_freq
    sgn = np.zeros((1, LANES), np.float32)
    sgn[0, :half] = -1.0
    sgn[0, half:d.rope] = 1.0
    tmq = d.tk
    invft = np.ascontiguousarray(np.broadcast_to(inv_freq[:, None], (half, tmq)))
    consts = (jnp.asarray(invf), jnp.asarray(sgn), jnp.asarray(invft))
    bcast = lambda v: jnp.broadcast_to(v.astype(F32)[:, None], (v.shape[0], tmq))

    x2d = x.reshape(t, dm)
    pos_f = positions.reshape(t).astype(F32)

    z = _inproj(x2d, g_mix.reshape(1, dm), w_in_r, d)
    qt, k, vt = _qkv(z, pos_f.reshape(t, 1), pos_f.reshape(1, t), consts,
                     g_q_latent.reshape(1, -1), g_kv_latent.reshape(1, -1),
                     bcast(g_q_nope), bcast(g_q_rope[:half]), bcast(g_q_rope[half:]),
                     _pad_lanes(g_k_nope), _pad_lanes(g_k_rope), wqt, wk, wvt, d, n_kr)
    ya, wgu_bf, wd_bf = _attn(qt, k, vt, w_gate_up.reshape(d.n_experts * dm, 2 * d.d_ff),
                              w_down.reshape(d.n_experts * d.d_ff, dm), d)
    wr = jnp.pad(w_router.astype(F32), ((0, 0), (0, LANES - d.n_experts)))
    wr_hi = wr.astype(BF16)
    wr = jnp.concatenate([wr_hi, (wr - wr_hi.astype(F32)).astype(BF16)], axis=1)
    x1, h4, logits = _mixout(z, ya, x2d, w_pool.astype(BF16), s_pool.reshape(1, dm), w_out.astype(BF16),
                             g_ffn.reshape(1, dm), wr, _pad_lanes(b_router), d, n_pool, n_ga, n_gp)

    dest, gates, counts = _route(logits, d)
    tm = d.tm_moe
    n_rows = t * TOP_K + d.n_experts * tm
    n_tiles = n_rows // tm
    cnt = counts[0, :d.n_experts].astype(jnp.int32)
    ends = jnp.cumsum((cnt + tm - 1) // tm * tm)
    n_used = ends[-1] // tm
    tile_ids = jnp.minimum(jnp.arange(n_tiles, dtype=jnp.int32), n_used - 1)
    tile_expert = jnp.sum(ends[None, :] <= (tile_ids * tm)[:, None], axis=1).astype(jnp.int32)
    tile_expert = jnp.minimum(tile_expert, d.n_experts - 1)
    starts = ends - (cnt + tm - 1) // tm * tm
    tile_valid = jnp.clip(starts[tile_expert] + cnt[tile_expert] - tile_ids * tm, 0, tm).astype(jnp.int32)

    dest_flat = dest.reshape(t * TOP_K)
    xs4 = _dispatch(dest_flat, h4, n_rows, d)
    y4 = _moe(tile_expert, tile_ids, tile_valid, n_used.reshape(1).astype(jnp.int32), xs4,
              wgu_bf, b_gate_up.reshape(d.n_experts, 1, 2 * d.d_ff),
              wd_bf.reshape(w_down.shape), b_down.reshape(d.n_experts, 1, dm), d)
    out = _combine(dest_flat, y4, gates, x1, d)
    return out.reshape(b, s, dm)


def kernel(x, positions, g_mix, w_in, g_q_latent, w_uq, g_kv_latent, w_ukv, g_q_nope, g_q_rope, g_k_nope, g_k_rope, w_pool, s_pool, w_out, g_ffn, w_router, b_router, w_gate_up, b_gate_up, w_down, b_down):
    return _layer(Dims(), x, positions, g_mix, w_in, g_q_latent, w_uq, g_kv_latent, w_ukv,
                  g_q_nope, g_q_rope, g_k_nope, g_k_rope, w_pool, s_pool, w_out,
                  g_ffn, w_router, b_router, w_gate_up, b_gate_up, w_down, b_down)
```
